```python
import math, functools
import jax, jax.numpy as jnp
from jax import lax
import numpy as np

D_MODEL = 2048
BATCH = 8
SEQ = 2048
DEPTH = 1
DEC_BATCH = 32
DEC_SEQ = 1
PAST_LEN = 16384
PAGE_SIZE = 128

GLA_HEADS = 4
GLA_DK = D_MODEL // (2 * GLA_HEADS)
GLA_DV = D_MODEL // GLA_HEADS
GLA_GATE_RANK = 16
GLA_GATE_TAU = 16.0
GLA_CHUNK = 64

DIFF_HEADS = 8
DIFF_HEAD_DIM = D_MODEL // (2 * DIFF_HEADS)
DIFF_VDIM = 2 * DIFF_HEAD_DIM
Q_BLOCK = 128

REL_BUCKETS = 32
REL_MAX_DIST = 128

MEM_TOKENS = 256
MEM_HEADS = 4
MEM_HEAD_DIM = D_MODEL // MEM_HEADS

D_FF = 5632
CONV_W = 3

N_BRANCH = 3
EPS = 1e-6

SPLIT_SIZES = (GLA_HEADS * GLA_DK, GLA_HEADS * GLA_DK, GLA_HEADS * GLA_DV, GLA_HEADS * GLA_DV, GLA_GATE_RANK,
               DIFF_HEADS * DIFF_VDIM, DIFF_HEADS * DIFF_VDIM, DIFF_HEADS * DIFF_VDIM,
               MEM_HEADS * MEM_HEAD_DIM, N_BRANCH * D_MODEL)
SPLIT_IDX = tuple(int(v) for v in np.cumsum(SPLIT_SIZES)[:-1])
D_IN = int(sum(SPLIT_SIZES))

kernel_name = "hybrid_gla_diffattn_memory_convffn_step"

F32 = jnp.float32


def rms_norm(x, g):
    xf = x.astype(F32)
    y = xf * lax.rsqrt(jnp.mean(xf * xf, axis=-1, keepdims=True) + EPS) * g.astype(F32)
    return y.astype(x.dtype)


def rel_bucket(dist):
    n = jnp.maximum(dist, 0)
    max_exact = REL_BUCKETS // 2
    large = max_exact + (jnp.log(jnp.maximum(n, 1).astype(F32) / max_exact)
                         / math.log(REL_MAX_DIST / max_exact) * (REL_BUCKETS - max_exact)).astype(jnp.int32)
    large = jnp.minimum(large, REL_BUCKETS - 1)
    return jnp.where(n < max_exact, n, large)


def rel_bias(q_pos, k_pos, table):
    b = rel_bucket(q_pos[:, None] - k_pos[None, :])
    return jnp.transpose(table[b].astype(F32), (2, 0, 1))


def gla_recurrent(q, k, v, log_a, s0):
    B, T = q.shape[:2]
    C = math.gcd(T, GLA_CHUNK)
    n = T // C

    def to_chunks(a):
        a = a.astype(F32)
        return a.reshape(B, n, C, a.shape[2], a.shape[3]).transpose(1, 0, 3, 2, 4)

    qc = to_chunks(q) * (GLA_DK ** -0.5)
    kc, vc, gc = to_chunks(k), to_chunks(v), to_chunks(log_a)
    causal = jnp.tril(jnp.ones((C, C), bool))[:, :, None]

    def step(S, inp):
        qi, ki, vi, gi = inp
        b = jnp.cumsum(gi, axis=2)
        o_inter = jnp.einsum('bhtk,bhkv->bhtv', qi * jnp.exp(b), S)
        diff = b[:, :, :, None, :] - b[:, :, None, :, :]
        decay = jnp.exp(jnp.where(causal, diff, -jnp.inf))
        A = jnp.einsum('bhtk,bhsk,bhtsk->bhts', qi, ki, decay)
        o_intra = jnp.einsum('bhts,bhsv->bhtv', A, vi)
        b_last = b[:, :, -1:, :]
        S_new = S * jnp.exp(b_last[:, :, 0, :, None]) + jnp.einsum('bhsk,bhsv->bhkv', ki * jnp.exp(b_last - b), vi)
        return S_new, o_inter + o_intra

    S_fin, o = lax.scan(step, s0.astype(F32), (qc, kc, vc, gc))
    o = o.transpose(1, 0, 3, 2, 4).reshape(B, T, q.shape[2], v.shape[3])
    return o, S_fin


def diff_attention_prompt(q, k, v, lam, rel_table):
    B, S = q.shape[:2]
    n_blk = S // Q_BLOCK
    scale = DIFF_HEAD_DIM ** -0.5
    kf, vf = k.astype(F32), v.astype(F32)
    qb = q.astype(F32).reshape(B, n_blk, Q_BLOCK, DIFF_HEADS, 2, DIFF_HEAD_DIM).swapaxes(0, 1)
    k_pos = jnp.arange(S, dtype=jnp.int32)

    def block(args):
        qi, i = args
        q_pos = i * Q_BLOCK + jnp.arange(Q_BLOCK, dtype=jnp.int32)
        s = jnp.einsum('bqhcd,bkhcd->bchqk', qi, kf) * scale + rel_bias(q_pos, k_pos, rel_table)
        s = jnp.where(k_pos[None, :] <= q_pos[:, None], s, -jnp.inf)
        p = jax.nn.softmax(s, axis=-1)
        w = p[:, 0] - lam * p[:, 1]
        return jnp.einsum('bhqk,bkhe->bqhe', w, vf)

    o = lax.map(block, (qb, jnp.arange(n_blk, dtype=jnp.int32)))
    return o.swapaxes(0, 1).reshape(B, S, DIFF_HEADS, DIFF_VDIM)


def diff_attention_sample(q, k_new, v_new, lam, cache_k, cache_v, layer, page_table, rel_table):
    Bd, T = q.shape[:2]
    n_pages = page_table.shape[1]
    qf = q.astype(F32) * (DIFF_HEAD_DIM ** -0.5)
    q_pos = PAST_LEN + jnp.arange(T, dtype=jnp.int32)

    def update(carry, s, v):
        m, l, acc = carry
        m_new = jnp.maximum(m, s.max(axis=-1))
        corr = jnp.exp(m - m_new)
        p = jnp.exp(s - m_new[..., None])
        return (m_new, l * corr + p.sum(axis=-1),
                acc * corr[..., None] + jnp.einsum('bchqk,bkhe->bchqe', p, v.astype(F32)))

    init = (jnp.full((Bd, 2, DIFF_HEADS, T), -jnp.inf, F32),
            jnp.zeros((Bd, 2, DIFF_HEADS, T), F32),
            jnp.zeros((Bd, 2, DIFF_HEADS, T, DIFF_VDIM), F32))

    def page_step(carry, j):
        phys = page_table[:, j]
        kp = cache_k[layer, phys].astype(F32).reshape(Bd, PAGE_SIZE, DIFF_HEADS, 2, DIFF_HEAD_DIM)
        vp = cache_v[layer, phys]
        k_pos = j * PAGE_SIZE + jnp.arange(PAGE_SIZE, dtype=jnp.int32)
        s = jnp.einsum('bqhcd,bkhcd->bchqk', qf, kp) + rel_bias(q_pos, k_pos, rel_table)
        return update(carry, s, vp), None

    carry, _ = lax.scan(page_step, init, jnp.arange(n_pages, dtype=jnp.int32))
    k_pos = PAST_LEN + jnp.arange(T, dtype=jnp.int32)
    s = jnp.einsum('bqhcd,bkhcd->bchqk', qf, k_new.astype(F32)) + rel_bias(q_pos, k_pos, rel_table)
    s = jnp.where(k_pos[None, :] <= q_pos[:, None], s, -jnp.inf)
    m, l, acc = update(carry, s, v_new)
    o = acc / l[..., None]
    out = o[:, 0] - lam * o[:, 1]
    return out.transpose(0, 2, 1, 3)


def memory_kv(mem, norm_mem, w_mem_kv):
    B, M, _ = mem.shape
    kv = (rms_norm(mem, norm_mem) @ w_mem_kv).reshape(B, M, 2, MEM_HEADS, MEM_HEAD_DIM)
    return kv[:, :, 0], kv[:, :, 1]


def memory_attention(q, mem_k, mem_v):
    s = jnp.einsum('bthd,bmhd->bhtm', q.astype(F32), mem_k.astype(F32)) * (MEM_HEAD_DIM ** -0.5)
    p = jax.nn.softmax(s, axis=-1)
    return jnp.einsum('bhtm,bmhd->bthd', p, mem_v.astype(F32))


def conv_ffn(h, conv_state, w_up, conv_w, conv_b, w_down):
    T = h.shape[1]
    u = h @ w_up
    ext = jnp.concatenate([conv_state.astype(u.dtype), u], axis=1)
    c = conv_b + sum(ext[:, i:i + T] * conv_w[i] for i in range(CONV_W))
    gate, val = jnp.split(c, 2, axis=-1)
    out = (jax.nn.gelu(gate, approximate=True) * val) @ w_down
    return out, ext[:, -(CONV_W - 1):]


def trunk_layer(x, diff_fn, gla_s0, mem_k, mem_v, conv_state, lam, lam_init,
                norm_pre_mix, norm_post_mix, norm_pre_ffn, norm_post_ffn, w_in, w_gla_gate2, b_gla_gate,
                gla_norm, diff_norm, w_out, w_up, ffn_conv_w, ffn_conv_b, w_down):
    B, T, _ = x.shape
    h = rms_norm(x, norm_pre_mix)
    qg, kg, vg, rg, ag, qd, kd, vd, qm, gates = jnp.split(h @ w_in, SPLIT_IDX, axis=-1)
    log_a = jax.nn.log_sigmoid((ag @ w_gla_gate2 + b_gla_gate).astype(F32)) / GLA_GATE_TAU
    o_gla, gla_state = gla_recurrent(qg.reshape(B, T, GLA_HEADS, GLA_DK), kg.reshape(B, T, GLA_HEADS, GLA_DK),
                                     vg.reshape(B, T, GLA_HEADS, GLA_DV), log_a.reshape(B, T, GLA_HEADS, GLA_DK),
                                     gla_s0)
    a = (rms_norm(o_gla, gla_norm) * jax.nn.silu(rg.reshape(B, T, GLA_HEADS, GLA_DV).astype(F32))).reshape(B, T, D_MODEL)
    k_rows = kd.reshape(B, T, DIFF_HEADS, DIFF_VDIM)
    v_rows = vd.reshape(B, T, DIFF_HEADS, DIFF_VDIM)
    o_diff = diff_fn(qd.reshape(B, T, DIFF_HEADS, 2, DIFF_HEAD_DIM),
                     kd.reshape(B, T, DIFF_HEADS, 2, DIFF_HEAD_DIM), v_rows, lam)
    b = (rms_norm(o_diff, diff_norm) * (1.0 - lam_init)).reshape(B, T, D_MODEL)
    c = memory_attention(qm.reshape(B, T, MEM_HEADS, MEM_HEAD_DIM), mem_k, mem_v).reshape(B, T, D_MODEL)
    g = jax.nn.sigmoid(gates.astype(F32)).reshape(B, T, N_BRANCH, D_MODEL)
    merged = g[:, :, 0] * a + g[:, :, 1] * b + g[:, :, 2] * c
    x = x + rms_norm(merged.astype(x.dtype) @ w_out, norm_post_mix)
    f, conv_new = conv_ffn(rms_norm(x, norm_pre_ffn), conv_state, w_up, ffn_conv_w, ffn_conv_b, w_down)
    x = x + rms_norm(f, norm_post_ffn)
    return x, k_rows, v_rows, gla_state, conv_new


def setup_inputs(seed: int = 0) -> dict:
    key = jax.random.key(seed)
    ks = jax.random.split(key, 40)
    n_pages = PAST_LEN // PAGE_SIZE
    n_used = DEC_BATCH * n_pages
    n_pool = n_used + max(1, n_used // 4)

    def nrm(k, shape, scale=1.0):
        return jax.random.normal(k, shape, F32) * scale

    def gain(k, shape):
        return 1.0 + 0.02 * jax.random.normal(k, shape, F32)

    page_table = jax.random.permutation(ks[9], n_pool)[:n_used].reshape(DEC_BATCH, n_pages).astype(jnp.int32)
    return {
        "x_prompt": nrm(ks[0], (BATCH, SEQ, D_MODEL)),
        "x_sample": nrm(ks[1], (DEC_BATCH, DEC_SEQ, D_MODEL)),
        "cache_k": nrm(ks[2], (DEPTH, n_pool, PAGE_SIZE, DIFF_HEADS, DIFF_VDIM)),
        "cache_v": nrm(ks[3], (DEPTH, n_pool, PAGE_SIZE, DIFF_HEADS, DIFF_VDIM)),
        "state_gla": nrm(ks[4], (DEPTH, DEC_BATCH, GLA_HEADS, GLA_DK, GLA_DV), 0.5),
        "cache_mem_k": nrm(ks[5], (DEPTH, DEC_BATCH, MEM_TOKENS, MEM_HEADS, MEM_HEAD_DIM)),
        "cache_mem_v": nrm(ks[6], (DEPTH, DEC_BATCH, MEM_TOKENS, MEM_HEADS, MEM_HEAD_DIM)),
        "state_ffn_conv": nrm(ks[7], (DEPTH, DEC_BATCH, CONV_W - 1, 2 * D_FF)),
        "page_table": page_table,
        "mem_prompt": nrm(ks[8], (BATCH, MEM_TOKENS, D_MODEL)),
        "rel_bias_table": nrm(ks[10], (REL_BUCKETS, DIFF_HEADS), 0.5),
        "norm_pre_mix": gain(ks[11], (DEPTH, D_MODEL)),
        "norm_post_mix": gain(ks[12], (DEPTH, D_MODEL)),
        "norm_pre_ffn": gain(ks[13], (DEPTH, D_MODEL)),
        "norm_post_ffn": gain(ks[14], (DEPTH, D_MODEL)),
        "norm_mem": gain(ks[15], (DEPTH, D_MODEL)),
        "w_in": nrm(ks[16], (DEPTH, D_MODEL, D_IN), D_MODEL ** -0.5),
        "w_gla_gate2": nrm(ks[17], (DEPTH, GLA_GATE_RANK, GLA_HEADS * GLA_DK), GLA_GATE_RANK ** -0.5),
        "b_gla_gate": nrm(ks[18], (DEPTH, GLA_HEADS * GLA_DK), 0.02),
        "gla_norm": gain(ks[19], (DEPTH, GLA_DV)),
        "diff_lambda_q1": nrm(ks[20], (DEPTH, DIFF_HEAD_DIM), 0.1),
        "diff_lambda_k1": nrm(ks[21], (DEPTH, DIFF_HEAD_DIM), 0.1),
        "diff_lambda_q2": nrm(ks[22], (DEPTH, DIFF_HEAD_DIM), 0.1),
        "diff_lambda_k2": nrm(ks[23], (DEPTH, DIFF_HEAD_DIM), 0.1),
        "diff_norm": gain(ks[24], (DEPTH, DIFF_VDIM)),
        "w_mem_kv": nrm(ks[25], (DEPTH, D_MODEL, 2 * MEM_HEADS * MEM_HEAD_DIM), D_MODEL ** -0.5),
        "w_out": nrm(ks[26], (DEPTH, D_MODEL, D_MODEL), D_MODEL ** -0.5),
        "w_up": nrm(ks[27], (DEPTH, D_MODEL, 2 * D_FF), D_MODEL ** -0.5),
        "ffn_conv_w": nrm(ks[28], (DEPTH, CONV_W, 2 * D_FF), CONV_W ** -0.5),
        "ffn_conv_b": nrm(ks[29], (DEPTH, 2 * D_FF), 0.02),
        "w_down": nrm(ks[30], (DEPTH, D_FF, D_MODEL), D_FF ** -0.5),
    }


def reference(x_prompt, x_sample, cache_k, cache_v, state_gla, cache_mem_k, cache_mem_v, state_ffn_conv,
              page_table, mem_prompt, rel_bias_table, norm_pre_mix, norm_post_mix, norm_pre_ffn, norm_post_ffn,
              norm_mem, w_in, w_gla_gate2, b_gla_gate, gla_norm, diff_lambda_q1, diff_lambda_k1, diff_lambda_q2,
              diff_lambda_k2, diff_norm, w_mem_kv, w_out, w_up, ffn_conv_w, ffn_conv_b, w_down):
    yp, ys = x_prompt, x_sample
    kp_l, vp_l, gp_l, mk_l, mv_l, cp_l = [], [], [], [], [], []
    ks_l, vs_l, gs_l, cs_l = [], [], [], []
    for l in range(DEPTH):
        lam_init = 0.8 - 0.6 * math.exp(-0.3 * l)
        lam = (jnp.exp(jnp.sum(diff_lambda_q1[l].astype(F32) * diff_lambda_k1[l].astype(F32)))
               - jnp.exp(jnp.sum(diff_lambda_q2[l].astype(F32) * diff_lambda_k2[l].astype(F32))) + lam_init)
        lw = (norm_pre_mix[l], norm_post_mix[l], norm_pre_ffn[l], norm_post_ffn[l], w_in[l], w_gla_gate2[l],
              b_gla_gate[l], gla_norm[l], diff_norm[l], w_out[l], w_up[l], ffn_conv_w[l], ffn_conv_b[l], w_down[l])
        mem_k, mem_v = memory_kv(mem_prompt, norm_mem[l], w_mem_kv[l])
        gla0 = jnp.zeros((yp.shape[0], GLA_HEADS, GLA_DK, GLA_DV), F32)
        conv0 = jnp.zeros((yp.shape[0], CONV_W - 1, 2 * D_FF), yp.dtype)
        diff_p = functools.partial(diff_attention_prompt, rel_table=rel_bias_table)
        yp, kr, vr, gst, cst = trunk_layer(yp, diff_p, gla0, mem_k, mem_v, conv0, lam, lam_init, *lw)
        kp_l.append(kr); vp_l.append(vr); gp_l.append(gst); mk_l.append(mem_k); mv_l.append(mem_v); cp_l.append(cst)
        diff_s = functools.partial(diff_attention_sample, cache_k=cache_k, cache_v=cache_v, layer=l,
                                   page_table=page_table, rel_table=rel_bias_table)
        ys, kr, vr, gst, cst = trunk_layer(ys, diff_s, state_gla[l], cache_mem_k[l], cache_mem_v[l],
                                           state_ffn_conv[l], lam, lam_init, *lw)
        ks_l.append(kr); vs_l.append(vr); gs_l.append(gst); cs_l.append(cst)
    k_rows_prompt, v_rows_prompt = jnp.stack(kp_l), jnp.stack(vp_l)
    gla_state_prompt, ffn_conv_prompt = jnp.stack(gp_l), jnp.stack(cp_l)
    mem_k_prompt, mem_v_prompt = jnp.stack(mk_l), jnp.stack(mv_l)
    k_rows_sample, v_rows_sample = jnp.stack(ks_l), jnp.stack(vs_l)
    gla_state_sample, ffn_conv_sample = jnp.stack(gs_l), jnp.stack(cs_l)
    return (yp, ys, k_rows_prompt, v_rows_prompt, gla_state_prompt, mem_k_prompt, mem_v_prompt, ffn_conv_prompt,
            k_rows_sample, v_rows_sample, gla_state_sample, ffn_conv_sample)
```

```python
import functools
import math

import jax
import jax.numpy as jnp
from jax import lax
from jax.experimental import pallas as pl
from jax.experimental.pallas import tpu as pltpu

F32 = jnp.float32
BF16 = jnp.bfloat16

GLA_HEADS = 4
GLA_GATE_TAU = 16.0
DIFF_HEADS = 8
MEM_HEADS = 4
REL_BUCKETS = 32
REL_MAX_DIST = 128
N_BRANCH = 3
EPS = 1e-6

NEG_BIG = -1e30
LANES = 128
VMEM_LIMIT_V7X = 56 * 1024 * 1024
GLA_BLOCK = 256
GLA_SUB = 16
ATTN_BLOCK = 256
PAGES_PER_STEP = 4
HIGHEST = lax.Precision.HIGHEST


def _cparams(*sem):
    return pltpu.CompilerParams(dimension_semantics=sem, vmem_limit_bytes=VMEM_LIMIT_V7X)


def _nt_dot(a, b):
    return lax.dot_general(a, b, (((1,), (1,)), ((), ())), preferred_element_type=F32)


def _tn_dot(a, b):
    return lax.dot_general(a, b, (((0,), (0,)), ((), ())), preferred_element_type=F32)


def _rms(x, g):
    return x * lax.rsqrt(jnp.mean(x * x, axis=-1, keepdims=True) + EPS) * g


def _sigmoid(x):
    return 1.0 / (1.0 + jnp.exp(-x))


def _norm_kernel(x_ref, g_ref, o_ref):
    o_ref[...] = _rms(x_ref[...], g_ref[...]).astype(o_ref.dtype)


def rmsnorm_cast(x, g, tm=512):
    m, d = x.shape
    tm = min(tm, m)
    assert m % tm == 0
    return pl.pallas_call(
        _norm_kernel,
        out_shape=jax.ShapeDtypeStruct((m, d), BF16),
        grid=(m // tm,),
        in_specs=[pl.BlockSpec((tm, d), lambda i: (i, 0)), pl.BlockSpec((1, d), lambda i: (0, 0))],
        out_specs=pl.BlockSpec((tm, d), lambda i: (i, 0)),
        compiler_params=_cparams("parallel"),
        name="rmsnorm_cast",
    )(x, g.reshape(1, d))


def _mm_kernel(x_ref, w_ref, *o_refs):
    acc = jnp.dot(x_ref[...], w_ref[...], preferred_element_type=F32)
    for o_ref in o_refs:
        o_ref[...] = acc.astype(o_ref.dtype)


def matmul(x, w, out_dtypes, tm=1024, tn=1024):
    m, k = x.shape
    n = w.shape[1]
    tm, tn = min(tm, m), min(tn, n)
    assert m % tm == 0 and n % tn == 0
    return pl.pallas_call(
        _mm_kernel,
        out_shape=[jax.ShapeDtypeStruct((m, n), dt) for dt in out_dtypes],
        grid=(m // tm, n // tn),
        in_specs=[pl.BlockSpec((tm, k), lambda i, j: (i, 0)), pl.BlockSpec((k, tn), lambda i, j: (0, j))],
        out_specs=[pl.BlockSpec((tm, tn), lambda i, j: (i, j)) for _ in out_dtypes],
        compiler_params=_cparams("parallel", "parallel"),
        name="matmul",
    )(x, w)


def _log_sigmoid(z):
    return -(jnp.maximum(-z, 0.0) + jnp.log1p(jnp.exp(-jnp.abs(z))))


def _gla_log_decay(ag, w2, gb):
    z = jnp.dot(ag, w2, preferred_element_type=F32, precision=HIGHEST)
    return _log_sigmoid(z + gb) * (1.0 / GLA_GATE_TAU)


def _gla_prompt_kernel(q_ref, k_ref, v_ref, r_ref, ag_ref, w2_ref, gb_ref, gn_ref, a_ref, s_ref,
                       b_scr, amat_scr, *, tb, dk):
    @pl.when(pl.program_id(2) == 0)
    def _():
        s_ref[...] = jnp.zeros_like(s_ref)

    scale = dk ** -0.5
    q = q_ref[...].astype(F32) * scale
    k = k_ref[...].astype(F32)
    v = v_ref[...]
    g = _gla_log_decay(ag_ref[...], w2_ref[...], gb_ref[...])
    row = lax.broadcasted_iota(jnp.int32, (tb, tb), 0)
    col = lax.broadcasted_iota(jnp.int32, (tb, tb), 1)
    tril = jnp.where(col <= row, 1.0, 0.0).astype(F32)
    b = jnp.dot(tril, g, preferred_element_type=F32, precision=HIGHEST)
    b_scr[...] = b
    b_last = b[tb - 1:tb, :]

    s_prev = s_ref[0, 0]
    o = jnp.dot((q * jnp.exp(b)).astype(BF16), s_prev.astype(BF16), preferred_element_type=F32)

    lane = lax.broadcasted_iota(jnp.int32, (GLA_SUB, tb), 1)
    lane_h = lax.broadcasted_iota(jnp.int32, (GLA_SUB, LANES), 1)
    row_h = lax.broadcasted_iota(jnp.int32, (GLA_SUB, LANES), 0)

    def sub_chunk(i, carry):
        base = pl.multiple_of(i * GLA_SUB, GLA_SUB)
        q_i = q_ref[pl.ds(base, GLA_SUB), :].astype(F32) * scale
        k_i = k_ref[pl.ds(base, GLA_SUB), :].astype(F32)
        b_i = b_scr[pl.ds(base, GLA_SUB), :]
        r_i = b_i[0:1, :]
        q_t = (q_i * jnp.exp(b_i - r_i)).astype(BF16)
        k_t = (k_ref[...].astype(F32) * jnp.exp(jnp.minimum(r_i - b_scr[...], 0.0))).astype(BF16)
        p = jnp.where(lane < base, _nt_dot(q_t, k_t), 0.0)
        lane0 = base % LANES
        d = jnp.zeros((GLA_SUB, LANES), F32)
        for s in range(GLA_SUB):
            e = jnp.exp(jnp.minimum(b_i - b_i[s:s + 1, :], 0.0))
            c = jnp.sum(q_i * e * k_i[s:s + 1, :], axis=-1, keepdims=True)
            d = d + jnp.where(lane_h == lane0 + s, jnp.where(row_h >= s, c, 0.0), 0.0)
        halves = [jnp.where(base // LANES == h, d, 0.0) for h in range(tb // LANES)]
        amat_scr[pl.ds(base, GLA_SUB), :] = p + jnp.concatenate(halves, axis=1)
        return carry

    lax.fori_loop(0, tb // GLA_SUB, sub_chunk, 0)
    o = o + jnp.dot(amat_scr[...].astype(BF16), v, preferred_element_type=F32)

    ones = jnp.ones((tb, LANES), F32)
    decay_col = jnp.exp(lax.dot_general(g, ones, (((0,), (0,)), ((), ())), preferred_element_type=F32,
                                        precision=HIGHEST))
    dv = v.shape[1]
    decay_full = jnp.concatenate([decay_col] * (dv // LANES), axis=1)
    k_hat = (k * jnp.exp(b_last - b)).astype(BF16)
    s_ref[0, 0] = s_prev * decay_full + _tn_dot(k_hat, v)

    rgate = r_ref[...].astype(F32)
    a_ref[...] = (_rms(o, gn_ref[...]) * (rgate * _sigmoid(rgate))).astype(a_ref.dtype)


def gla_prompt(qkvr, ag, w2, gb, gnorm, batch, seq, dk, dv):
    heads = GLA_HEADS
    tb = min(GLA_BLOCK, seq)
    nblk = seq // tb
    row = lambda b, h, i: b * nblk + i
    kern = functools.partial(_gla_prompt_kernel, tb=tb, dk=dk)
    return pl.pallas_call(
        kern,
        out_shape=[jax.ShapeDtypeStruct((batch * seq, heads * dv), BF16),
                   jax.ShapeDtypeStruct((batch, heads, dk, dv), F32)],
        grid=(batch, heads, nblk),
        in_specs=[
            pl.BlockSpec((tb, dk), lambda b, h, i: (row(b, h, i), h)),
            pl.BlockSpec((tb, dk), lambda b, h, i: (row(b, h, i), heads + h)),
            pl.BlockSpec((tb, dv), lambda b, h, i: (row(b, h, i), (2 * heads * dk) // dv + h)),
            pl.BlockSpec((tb, dv), lambda b, h, i: (row(b, h, i), (2 * heads * dk) // dv + heads + h)),
            pl.BlockSpec((tb, LANES), lambda b, h, i: (row(b, h, i), 0)),
            pl.BlockSpec((LANES, dk), lambda b, h, i: (0, h)),
            pl.BlockSpec((1, dk), lambda b, h, i: (0, h)),
            pl.BlockSpec((1, dv), lambda b, h, i: (0, 0)),
        ],
        out_specs=[pl.BlockSpec((tb, dv), lambda b, h, i: (row(b, h, i), h)),
                   pl.BlockSpec((1, 1, dk, dv), lambda b, h, i: (b, h, 0, 0))],
        scratch_shapes=[pltpu.VMEM((tb, dk), F32), pltpu.VMEM((tb, tb), F32)],
        compiler_params=_cparams("parallel", "parallel", "arbitrary"),
        name="gla_prompt",
    )(qkvr, qkvr, qkvr, qkvr, ag, w2, gb.reshape(1, -1), gnorm.reshape(1, -1))


def _gla_step_kernel(q_ref, k_ref, v_ref, r_ref, ag_ref, w2_ref, gb_ref, gn_ref, s0_ref, a_ref, s_ref, *, nb, dk):
    g = _gla_log_decay(ag_ref[...], w2_ref[...], gb_ref[...])
    reps = LANES // nb
    decay = jnp.exp(g)
    decay_t = jnp.concatenate([decay] * reps, axis=0).T
    k = k_ref[...]
    k_t = jnp.concatenate([k] * reps, axis=0).T
    q = q_ref[...] * (dk ** -0.5)
    q_dec = (q * decay).astype(BF16)
    v = v_ref[...]
    rows = lax.broadcasted_iota(jnp.int32, (nb, v.shape[1]), 0)
    o = jnp.sum(q * k, axis=-1, keepdims=True) * v
    for j in range(nb):
        s0 = s0_ref[j, 0]
        s_ref[j, 0] = s0 * decay_t[:, j:j + 1] + k_t[:, j:j + 1] * v[j:j + 1, :]
        o = o + jnp.where(rows == j, jnp.dot(q_dec, s0.astype(BF16), preferred_element_type=F32), 0.0)
    rgate = r_ref[...]
    a_ref[...] = _rms(o, gn_ref[...]) * (rgate * _sigmoid(rgate))


def gla_step(qkvr, ag, w2, gb, gnorm, s0, dk, dv):
    heads = GLA_HEADS
    batch = qkvr.shape[0]
    nb = 8
    kern = functools.partial(_gla_step_kernel, nb=nb, dk=dk)
    return pl.pallas_call(
        kern,
        out_shape=[jax.ShapeDtypeStruct((batch, heads * dv), F32),
                   jax.ShapeDtypeStruct((batch, heads, dk, dv), F32)],
        grid=(batch // nb, heads),
        in_specs=[
            pl.BlockSpec((nb, dk), lambda b, h: (b, h)),
            pl.BlockSpec((nb, dk), lambda b, h: (b, heads + h)),
            pl.BlockSpec((nb, dv), lambda b, h: (b, (2 * heads * dk) // dv + h)),
            pl.BlockSpec((nb, dv), lambda b, h: (b, (2 * heads * dk) // dv + heads + h)),
            pl.BlockSpec((nb, LANES), lambda b, h: (b, 0)),
            pl.BlockSpec((LANES, dk), lambda b, h: (0, h)),
            pl.BlockSpec((1, dk), lambda b, h: (0, h)),
            pl.BlockSpec((1, dv), lambda b, h: (0, 0)),
            pl.BlockSpec((nb, 1, dk, dv), lambda b, h: (b, h, 0, 0)),
        ],
        out_specs=[pl.BlockSpec((nb, dv), lambda b, h: (b, h)),
                   pl.BlockSpec((nb, 1, dk, dv), lambda b, h: (b, h, 0, 0))],
        compiler_params=_cparams("parallel", "parallel"),
        name="gla_step",
    )(qkvr, qkvr, qkvr, qkvr, ag, w2, gb.reshape(1, -1), gnorm.reshape(1, -1), s0)


def _rel_bucket(dist):
    n = jnp.maximum(dist, 0)
    max_exact = REL_BUCKETS // 2
    large = max_exact + (jnp.log(jnp.maximum(n, 1).astype(F32) / max_exact)
                         / math.log(REL_MAX_DIST / max_exact) * (REL_BUCKETS - max_exact)).astype(jnp.int32)
    large = jnp.minimum(large, REL_BUCKETS - 1)
    return jnp.where(n < max_exact, n, large)


def _diff_prompt_kernel(lam_ref, q_ref, k_ref, v_ref, bias_ref, far_ref, dn_ref, o_ref,
                        m_scr, l_scr, acc_scr, *, tq, dh, out_scale):
    qi = pl.program_id(2)
    h = pl.program_id(1)
    scale = dh ** -0.5
    m_scr[...] = jnp.full_like(m_scr, NEG_BIG)
    l_scr[...] = jnp.zeros_like(l_scr)
    acc_scr[...] = jnp.zeros_like(acc_scr)
    q = q_ref[...]

    def visit(kb, bias):
        start = pl.multiple_of(kb * tq, tq)
        kk = k_ref[pl.ds(start, tq), :]
        vv = v_ref[pl.ds(start, tq), :]
        for c in range(2):
            s = _nt_dot(q[:, c * dh:(c + 1) * dh], kk[:, c * dh:(c + 1) * dh]) * scale + bias
            m_old = m_scr[c]
            m_new = jnp.maximum(m_old, jnp.max(s, axis=-1, keepdims=True))
            corr = jnp.exp(m_old - m_new)
            p = jnp.exp(s - m_new)
            l_scr[c] = l_scr[c] * corr + jnp.sum(p, axis=-1, keepdims=True)
            acc_scr[c] = acc_scr[c] * corr + jnp.dot(p.astype(BF16), vv, preferred_element_type=F32)
            m_scr[c] = m_new

    far = far_ref[h]

    def far_block(kb, carry):
        visit(kb, far)
        return carry

    lax.fori_loop(0, jnp.maximum(qi - 1, 0), far_block, 0)

    @pl.when(qi >= 1)
    def _():
        visit(qi - 1, bias_ref[0, 1])

    visit(qi, bias_ref[0, 0])
    lam = lam_ref[0]
    o = acc_scr[0] / l_scr[0] - lam * (acc_scr[1] / l_scr[1])
    o_ref[...] = (_rms(o, dn_ref[...]) * out_scale).astype(o_ref.dtype)


def diff_attention_prompt(q, k, v, lam, rel_table, diff_norm, batch, seq, lam_init):
    heads = DIFF_HEADS
    dh = q.shape[1] // (2 * heads)
    tq = min(ATTN_BLOCK, seq)
    assert tq >= REL_MAX_DIST or seq == tq
    nq = seq // tq
    r = jnp.arange(tq, dtype=jnp.int32)
    dist = r[:, None] - r[None, :]
    bias = jnp.stack([rel_table[_rel_bucket(dist)], rel_table[_rel_bucket(dist + tq)]])
    bias = jnp.transpose(bias, (3, 0, 1, 2)).astype(F32)
    bias = bias.at[:, 0].set(jnp.where(dist >= 0, bias[:, 0], NEG_BIG))
    far = rel_table[REL_BUCKETS - 1].astype(F32)
    kern = functools.partial(_diff_prompt_kernel, tq=tq, dh=dh, out_scale=1.0 - lam_init)
    return pl.pallas_call(
        kern,
        out_shape=jax.ShapeDtypeStruct((batch * seq, heads * 2 * dh), BF16),
        grid=(batch, heads, nq),
        in_specs=[
            pl.BlockSpec(memory_space=pltpu.SMEM),
            pl.BlockSpec((tq, 2 * dh), lambda b, h, i: (b * nq + i, h)),
            pl.BlockSpec((seq, 2 * dh), lambda b, h, i: (b, h)),
            pl.BlockSpec((seq, 2 * dh), lambda b, h, i: (b, h)),
            pl.BlockSpec((1, 2, tq, tq), lambda b, h, i: (h, 0, 0, 0)),
            pl.BlockSpec(memory_space=pltpu.SMEM),
            pl.BlockSpec((1, 2 * dh), lambda b, h, i: (0, 0)),
        ],
        out_specs=pl.BlockSpec((tq, 2 * dh), lambda b, h, i: (b * nq + i, h)),
        scratch_shapes=[pltpu.VMEM((2, tq, 1), F32), pltpu.VMEM((2, tq, 1), F32), pltpu.VMEM((2, tq, 2 * dh), F32)],
        compiler_params=_cparams("parallel", "parallel", "arbitrary"),
        name="diff_attention_prompt",
    )(lam.reshape(1), q, k, v, bias, far, diff_norm.reshape(1, -1))


def _diff_paged_kernel(pt_ref, lam_ref, q_ref, *rest, n_pages, pages, out_scale, heads):
    k_refs = rest[:pages]
    v_refs = rest[pages:2 * pages]
    bias_ref, kn_ref, vn_ref, bn_ref, dn_ref, o_ref, m_scr, l_scr, acc_scr = rest[2 * pages:]
    del pt_ref
    jg = pl.program_id(1)

    @pl.when(jg == 0)
    def _():
        m_scr[...] = jnp.full_like(m_scr, NEG_BIG)
        l_scr[...] = jnp.zeros_like(l_scr)
        acc_scr[...] = jnp.zeros_like(acc_scr)

    q = q_ref[0]

    def update(kk, vv, bias):
        s = _nt_dot(q, kk.astype(BF16)) + bias
        m_old = m_scr[...]
        m_new = jnp.maximum(m_old, jnp.max(s, axis=-1, keepdims=True))
        corr = jnp.exp(m_old - m_new)
        p = jnp.exp(s - m_new)
        l_scr[...] = l_scr[...] * corr + jnp.sum(p, axis=-1, keepdims=True)
        acc_scr[...] = acc_scr[...] * corr + jnp.dot(p.astype(BF16), vv.astype(BF16), preferred_element_type=F32)
        m_scr[...] = m_new

    for i in range(pages):
        kp = k_refs[i][0, 0]
        vp = v_refs[i][0, 0]
        rows = kp.shape[0] * kp.shape[1]
        is_last = (jg * pages + i == n_pages - 1).astype(jnp.int32)
        update(kp.reshape(rows, kp.shape[2]), vp.reshape(rows, vp.shape[2]), bias_ref[is_last])

    @pl.when(jg == pl.num_programs(1) - 1)
    def _():
        update(kn_ref[0], vn_ref[0], bn_ref[...])
        o = acc_scr[...] / l_scr[...]
        out = o[:heads] - lam_ref[0] * o[heads:]
        o_ref[0] = _rms(out, dn_ref[...]) * out_scale


def diff_attention_paged(q, k_new, v_new, lam, cache_k, cache_v, layer, page_table, rel_table, diff_norm, lam_init):
    heads = DIFF_HEADS
    batch = q.shape[0]
    dh = q.shape[1] // (2 * heads)
    n_pages = page_table.shape[1]
    page = cache_k.shape[2]
    past = n_pages * page
    pages = math.gcd(PAGES_PER_STEP, n_pages)
    assert page >= REL_MAX_DIST
    rows = page * heads
    q4 = (q.reshape(batch, heads, 2, dh) * (dh ** -0.5)).astype(BF16)
    zeros = jnp.zeros_like(q4[:, :, 0])
    q_mat = jnp.concatenate([jnp.concatenate([q4[:, :, 0], zeros], axis=-1),
                             jnp.concatenate([zeros, q4[:, :, 1]], axis=-1)], axis=1)
    tok = jnp.arange(page, dtype=jnp.int32)
    dist_last = past - ((n_pages - 1) * page + tok)
    b_far = jnp.broadcast_to(rel_table[REL_BUCKETS - 1].astype(F32)[:, None], (heads, page))
    b_last = rel_table[_rel_bucket(dist_last)].astype(F32).T
    own = (jnp.arange(heads)[:, None, None] == jnp.arange(heads)[None, None, :])
    def expand(bh):
        t = jnp.where(own, bh[:, :, None], NEG_BIG).reshape(heads, rows)
        return jnp.concatenate([t, t], axis=0)
    bias = jnp.stack([expand(b_far), expand(b_last)])
    pad = LANES - heads
    kn = jnp.pad(k_new.reshape(batch, heads, 2 * dh), ((0, 0), (0, pad), (0, 0)))
    vn = jnp.pad(v_new.reshape(batch, heads, 2 * dh), ((0, 0), (0, pad), (0, 0)))
    b0 = jnp.where(jnp.arange(heads)[:, None] == jnp.arange(LANES)[None, :],
                   rel_table[0].astype(F32)[:, None], NEG_BIG)
    bn = jnp.concatenate([b0, b0], axis=0)

    def page_spec(i):
        return pl.BlockSpec((1, 1) + cache_k.shape[2:],
                            lambda b, j, pt: (layer, pt[b * n_pages + j * pages + i], 0, 0, 0))

    kern = functools.partial(_diff_paged_kernel, n_pages=n_pages, pages=pages, out_scale=1.0 - lam_init, heads=heads)
    out = pl.pallas_call(
        kern,
        out_shape=jax.ShapeDtypeStruct((batch, heads, 2 * dh), F32),
        grid_spec=pltpu.PrefetchScalarGridSpec(
            num_scalar_prefetch=1,
            grid=(batch, n_pages // pages),
            in_specs=[pl.BlockSpec(memory_space=pltpu.SMEM),
                      pl.BlockSpec((1, 2 * heads, 2 * dh), lambda b, j, pt: (b, 0, 0))]
                     + [page_spec(i) for i in range(pages)] + [page_spec(i) for i in range(pages)]
                     + [pl.BlockSpec((2, 2 * heads, rows), lambda b, j, pt: (0, 0, 0)),
                        pl.BlockSpec((1, LANES, 2 * dh), lambda b, j, pt: (b, 0, 0)),
                        pl.BlockSpec((1, LANES, 2 * dh), lambda b, j, pt: (b, 0, 0)),
                        pl.BlockSpec((2 * heads, LANES), lambda b, j, pt: (0, 0)),
                        pl.BlockSpec((1, 2 * dh), lambda b, j, pt: (0, 0))],
            out_specs=pl.BlockSpec((1, heads, 2 * dh), lambda b, j, pt: (b, 0, 0)),
            scratch_shapes=[pltpu.VMEM((2 * heads, 1), F32), pltpu.VMEM((2 * heads, 1), F32),
                            pltpu.VMEM((2 * heads, 2 * dh), F32)],
        ),
        compiler_params=_cparams("parallel", "arbitrary"),
        name="diff_attention_paged",
    )(page_table.reshape(-1), lam.reshape(1), q_mat, *([cache_k] * pages), *([cache_v] * pages),
      bias, kn, vn, bn, diff_norm.reshape(1, -1))
    return out.reshape(batch, heads * 2 * dh)


def _mem_attn_kernel(q_ref, k_ref, v_ref, o_ref, *, scale):
    q = q_ref[0]
    tq = q.shape[0]
    if tq < 8:
        q = jnp.broadcast_to(q[0:1], (8, q.shape[1]))
    s = _nt_dot(q.astype(BF16), k_ref[0].astype(BF16)) * scale
    p = jnp.exp(s - jnp.max(s, axis=-1, keepdims=True))
    o = jnp.dot(p.astype(BF16), v_ref[0].astype(BF16), preferred_element_type=F32)
    o = o / jnp.sum(p, axis=-1, keepdims=True)
    o_ref[0] = o[:tq].astype(o_ref.dtype)


def memory_attention(q3, col0, mem_k, mem_v, out_dtype, tq=512):
    batch, seq, _ = q3.shape
    mtok, width = mem_k.shape[1:]
    md = width // MEM_HEADS
    tq = min(tq, seq)
    kern = functools.partial(_mem_attn_kernel, scale=md ** -0.5)
    return pl.pallas_call(
        kern,
        out_shape=jax.ShapeDtypeStruct((batch, seq, width), out_dtype),
        grid=(batch, MEM_HEADS, seq // tq),
        in_specs=[pl.BlockSpec((1, tq, md), lambda b, h, i: (b, i, col0 + h)),
                  pl.BlockSpec((1, mtok, md), lambda b, h, i: (b, 0, h)),
                  pl.BlockSpec((1, mtok, md), lambda b, h, i: (b, 0, h))],
        out_specs=pl.BlockSpec((1, tq, md), lambda b, h, i: (b, i, h)),
        compiler_params=_cparams("parallel", "parallel", "parallel"),
        name="memory_attention",
    )(q3, mem_k, mem_v)


def _merge_kernel(a_ref, b_ref, c_ref, g0_ref, g1_ref, g2_ref, x_ref, w_ref, npost_ref, npre_ref, x1_ref, h2_ref):
    merged = (_sigmoid(g0_ref[...].astype(F32)) * a_ref[...].astype(F32)
              + _sigmoid(g1_ref[...].astype(F32)) * b_ref[...].astype(F32)
              + _sigmoid(g2_ref[...].astype(F32)) * c_ref[...].astype(F32))
    y = jnp.dot(merged.astype(BF16), w_ref[...], preferred_element_type=F32)
    x1 = x_ref[...] + _rms(y, npost_ref[...])
    x1_ref[...] = x1
    h2_ref[...] = _rms(x1, npre_ref[...]).astype(h2_ref.dtype)


def merge_out_proj(a, b, c, gates_arr, gate_col0, x, w_out, norm_post, norm_pre_ffn, tm=256):
    m, d = x.shape
    tm = min(tm, m)
    rowblk = lambda i: (i, 0)
    vec = pl.BlockSpec((1, d), lambda i: (0, 0))
    return pl.pallas_call(
        _merge_kernel,
        out_shape=[jax.ShapeDtypeStruct((m, d), F32), jax.ShapeDtypeStruct((m, d), BF16)],
        grid=(m // tm,),
        in_specs=[pl.BlockSpec((tm, d), rowblk), pl.BlockSpec((tm, d), rowblk), pl.BlockSpec((tm, d), rowblk),
                  pl.BlockSpec((tm, d), lambda i: (i, gate_col0)),
                  pl.BlockSpec((tm, d), lambda i: (i, gate_col0 + 1)),
                  pl.BlockSpec((tm, d), lambda i: (i, gate_col0 + 2)),
                  pl.BlockSpec((tm, d), rowblk),
                  pl.BlockSpec((d, d), lambda i: (0, 0)),
                  vec, vec],
        out_specs=[pl.BlockSpec((tm, d), rowblk), pl.BlockSpec((tm, d), rowblk)],
        compiler_params=_cparams("parallel"),
        name="merge_out_proj",
    )(a, b, c, gates_arr, gates_arr, gates_arr, x, w_out, norm_post.reshape(1, d), norm_pre_ffn.reshape(1, d))


def _gelu_tanh(x):
    return 0.5 * x * (1.0 + jnp.tanh(math.sqrt(2.0 / math.pi) * (x + 0.044715 * (x * x * x))))


def _ffn_up_seq_kernel(h_ref, wg_ref, wv_ref, cwg_ref, cwv_ref, cbg_ref, cbv_ref, stg_ref, stv_ref,
                       act_ref, tailg_ref, tailv_ref, carryg, carryv, *, tiles_per_seq):
    mi = pl.program_id(1)
    h = h_ref[...]
    tm = h.shape[0]
    row = lax.broadcasted_iota(jnp.int32, (tm, 1), 0)
    seq_start = mi % tiles_per_seq == 0

    def conv(w_ref, cw_ref, cb_ref, st_ref, carry, tail_ref):
        u = jnp.dot(h, w_ref[...], preferred_element_type=F32)
        st = st_ref[0]
        prev2 = jnp.where(seq_start, st[0:1, :], carry[6:7, :])
        prev1 = jnp.where(seq_start, st[1:2, :], carry[7:8, :])
        u1 = jnp.where(row >= 1, pltpu.roll(u, 1, 0), prev1)
        u2 = jnp.where(row >= 2, pltpu.roll(u, 2, 0), jnp.where(row == 1, prev1, prev2))
        cw = cw_ref[...]
        tail = u[tm - 8:tm, :]
        carry[...] = tail
        tail_ref[0] = tail
        return cb_ref[...] + u2 * cw[0:1, :] + u1 * cw[1:2, :] + u * cw[2:3, :]

    gate = conv(wg_ref, cwg_ref, cbg_ref, stg_ref, carryg, tailg_ref)
    val = conv(wv_ref, cwv_ref, cbv_ref, stv_ref, carryv, tailv_ref)
    act_ref[...] = (_gelu_tanh(gate) * val).astype(act_ref.dtype)


def ffn_up_seq(h2, w_up, conv_w, conv_b, conv_state, seq, tm=1024, tn=512):
    m, d = h2.shape
    f = w_up.shape[1] // 2
    tm = min(tm, seq)
    tiles_per_seq = seq // tm
    nj = f // tn
    kern = functools.partial(_ffn_up_seq_kernel, tiles_per_seq=tiles_per_seq)
    cw = conv_w.astype(F32)
    cb = conv_b.reshape(1, -1).astype(F32)
    return pl.pallas_call(
        kern,
        out_shape=[jax.ShapeDtypeStruct((m, f), BF16),
                   jax.ShapeDtypeStruct((m // tm, 8, f), F32),
                   jax.ShapeDtypeStruct((m // tm, 8, f), F32)],
        grid=(nj, m // tm),
        in_specs=[pl.BlockSpec((tm, d), lambda j, i: (i, 0)),
                  pl.BlockSpec((d, tn), lambda j, i: (0, j)),
                  pl.BlockSpec((d, tn), lambda j, i: (0, nj + j)),
                  pl.BlockSpec((cw.shape[0], tn), lambda j, i: (0, j)),
                  pl.BlockSpec((cw.shape[0], tn), lambda j, i: (0, nj + j)),
                  pl.BlockSpec((1, tn), lambda j, i: (0, j)),
                  pl.BlockSpec((1, tn), lambda j, i: (0, nj + j)),
                  pl.BlockSpec((1, 2, tn), lambda j, i: (i // tiles_per_seq, 0, j)),
                  pl.BlockSpec((1, 2, tn), lambda j, i: (i // tiles_per_seq, 0, nj + j))],
        out_specs=[pl.BlockSpec((tm, tn), lambda j, i: (i, j)),
                   pl.BlockSpec((1, 8, tn), lambda j, i: (i, 0, j)),
                   pl.BlockSpec((1, 8, tn), lambda j, i: (i, 0, j))],
        scratch_shapes=[pltpu.VMEM((8, tn), F32), pltpu.VMEM((8, tn), F32)],
        compiler_params=_cparams("parallel", "arbitrary"),
        name="ffn_up_seq",
    )(h2, w_up, w_up, cw, cw, cb, cb, conv_state, conv_state)


def _ffn_up_step_kernel(h_ref, wg_ref, wv_ref, cwg_ref, cwv_ref, cbg_ref, cbv_ref, stg_ref, stv_ref,
                        act_ref, ug_ref, uv_ref):
    h = h_ref[...]

    def conv(w_ref, cw_ref, cb_ref, st_ref, u_ref):
        u = jnp.dot(h, w_ref[...], preferred_element_type=F32)
        u_ref[...] = u
        cw = cw_ref[...]
        return cb_ref[...] + st_ref[0] * cw[0:1, :] + st_ref[1] * cw[1:2, :] + u * cw[2:3, :]

    gate = conv(wg_ref, cwg_ref, cbg_ref, stg_ref, ug_ref)
    val = conv(wv_ref, cwv_ref, cbv_ref, stv_ref, uv_ref)
    act_ref[...] = (_gelu_tanh(gate) * val).astype(act_ref.dtype)


def ffn_up_step(h2, w_up, conv_w, conv_b, conv_state_t, tn=512):
    m, d = h2.shape
    f = w_up.shape[1] // 2
    nj = f // tn
    cw = conv_w.astype(F32)
    cb = conv_b.reshape(1, -1).astype(F32)
    return pl.pallas_call(
        _ffn_up_step_kernel,
        out_shape=[jax.ShapeDtypeStruct((m, f), BF16), jax.ShapeDtypeStruct((m, f), F32),
                   jax.ShapeDtypeStruct((m, f), F32)],
        grid=(nj,),
        in_specs=[pl.BlockSpec((m, d), lambda j: (0, 0)),
                  pl.BlockSpec((d, tn), lambda j: (0, j)),
                  pl.BlockSpec((d, tn), lambda j: (0, nj + j)),
                  pl.BlockSpec((cw.shape[0], tn), lambda j: (0, j)),
                  pl.BlockSpec((cw.shape[0], tn), lambda j: (0, nj + j)),
                  pl.BlockSpec((1, tn), lambda j: (0, j)),
                  pl.BlockSpec((1, tn), lambda j: (0, nj + j)),
                  pl.BlockSpec((2, m, tn), lambda j: (0, 0, j)),
                  pl.BlockSpec((2, m, tn), lambda j: (0, 0, nj + j))],
        out_specs=[pl.BlockSpec((m, tn), lambda j: (0, j)),
                   pl.BlockSpec((m, tn), lambda j: (0, j)),
                   pl.BlockSpec((m, tn), lambda j: (0, j))],
        compiler_params=_cparams("parallel"),
        name="ffn_up_step",
    )(h2, w_up, w_up, cw, cw, cb, cb, conv_state_t, conv_state_t)


def _ffn_down_kernel(act_ref, w_ref, x_ref, n_ref, y_ref, acc_scr):
    kk = pl.program_id(1)

    @pl.when(kk == 0)
    def _():
        acc_scr[...] = jnp.zeros_like(acc_scr)

    acc_scr[...] += jnp.dot(act_ref[...], w_ref[...], preferred_element_type=F32)

    @pl.when(kk == pl.num_programs(1) - 1)
    def _():
        y_ref[...] = x_ref[...] + _rms(acc_scr[...], n_ref[...])


def ffn_down(act, w_down, x1, norm_post, tm=512, tk=512):
    m, f = act.shape
    d = w_down.shape[1]
    tm = min(tm, m)
    return pl.pallas_call(
        _ffn_down_kernel,
        out_shape=jax.ShapeDtypeStruct((m, d), F32),
        grid=(m // tm, f // tk),
        in_specs=[pl.BlockSpec((tm, tk), lambda i, k: (i, k)),
                  pl.BlockSpec((tk, d), lambda i, k: (k, 0)),
                  pl.BlockSpec((tm, d), lambda i, k: (i, 0)),
                  pl.BlockSpec((1, d), lambda i, k: (0, 0))],
        out_specs=pl.BlockSpec((tm, d), lambda i, k: (i, 0)),
        scratch_shapes=[pltpu.VMEM((tm, d), F32)],
        compiler_params=_cparams("parallel", "arbitrary"),
        name="ffn_down",
    )(act, w_down, x1, norm_post.reshape(1, d))


def _split_weights(w_in, w2, d):
    gk = d // 2
    rank = w2.shape[0]
    o_ag = 2 * gk + 2 * d
    o_diff = o_ag + rank
    wb = w_in.astype(BF16)
    return dict(
        gla=wb[:, :o_ag],
        ag=jnp.pad(wb[:, o_ag:o_diff], ((0, 0), (0, LANES - rank))),
        qd=wb[:, o_diff:o_diff + d],
        kd=wb[:, o_diff + d:o_diff + 2 * d],
        vd=wb[:, o_diff + 2 * d:o_diff + 3 * d],
        qm_gates=wb[:, o_diff + 3 * d:],
        w2=jnp.pad(w2.astype(F32), ((0, LANES - rank), (0, 0))),
    )


def kernel(x_prompt, x_sample, cache_k, cache_v, state_gla, cache_mem_k, cache_mem_v, state_ffn_conv, page_table, mem_prompt, rel_bias_table, norm_pre_mix, norm_post_mix, norm_pre_ffn, norm_post_ffn, norm_mem, w_in, w_gla_gate2, b_gla_gate, gla_norm, diff_lambda_q1, diff_lambda_k1, diff_lambda_q2, diff_lambda_k2, diff_norm, w_mem_kv, w_out, w_up, ffn_conv_w, ffn_conv_b, w_down):
    depth = w_in.shape[0]
    bp, tp, d = x_prompt.shape
    bs, ts, _ = x_sample.shape
    assert ts == 1
    dk, dv = d // (2 * GLA_HEADS), d // GLA_HEADS
    mtok = mem_prompt.shape[1]
    f2 = w_up.shape[2]

    yp = x_prompt.reshape(bp * tp, d)
    ys = x_sample.reshape(bs * ts, d)
    outs = {n: [] for n in ("kp", "vp", "gp", "mk", "mv", "cp", "ks", "vs", "gs", "cs")}
    for l in range(depth):
        lam_init = 0.8 - 0.6 * math.exp(-0.3 * l)
        lam = (jnp.exp(jnp.sum(diff_lambda_q1[l].astype(F32) * diff_lambda_k1[l].astype(F32)))
               - jnp.exp(jnp.sum(diff_lambda_q2[l].astype(F32) * diff_lambda_k2[l].astype(F32))) + lam_init)
        wts = _split_weights(w_in[l], w_gla_gate2[l], d)
        w_out_b, w_up_b, w_down_b = w_out[l].astype(BF16), w_up[l].astype(BF16), w_down[l].astype(BF16)
        w_mem_b = w_mem_kv[l].astype(BF16)

        hm = rmsnorm_cast(mem_prompt.reshape(bp * mtok, d), norm_mem[l])
        (mem_k,) = matmul(hm, w_mem_b[:, :d], [F32])
        (mem_v,) = matmul(hm, w_mem_b[:, d:], [F32])
        h = rmsnorm_cast(yp, norm_pre_mix[l])
        (qkvr,) = matmul(h, wts["gla"], [BF16])
        (ag,) = matmul(h, wts["ag"], [F32], tn=LANES)
        (qd,) = matmul(h, wts["qd"], [BF16])
        k_rows, kd_b = matmul(h, wts["kd"], [F32, BF16])
        v_rows, vd_b = matmul(h, wts["vd"], [F32, BF16])
        (qm_gates,) = matmul(h, wts["qm_gates"], [BF16])
        a, gla_state = gla_prompt(qkvr, ag, wts["w2"], b_gla_gate[l], gla_norm[l], bp, tp, dk, dv)
        b = diff_attention_prompt(qd, kd_b, vd_b, lam, rel_bias_table, diff_norm[l], bp, tp, lam_init)
        c = memory_attention(qm_gates.reshape(bp, tp, -1), 0, mem_k.reshape(bp, mtok, d),
                             mem_v.reshape(bp, mtok, d), BF16)
        x1, h2 = merge_out_proj(a, b, c.reshape(bp * tp, d), qm_gates, 1, yp, w_out_b, norm_post_mix[l],
                                norm_pre_ffn[l])
        conv0 = jnp.zeros((bp, ffn_conv_w.shape[1] - 1, f2), F32)
        act, tail_g, tail_v = ffn_up_seq(h2, w_up_b, ffn_conv_w[l], ffn_conv_b[l], conv0, tp)
        yp = ffn_down(act, w_down_b, x1, norm_post_ffn[l])
        tiles = tail_g.shape[0] // bp
        tail = jnp.concatenate([tail_g, tail_v], axis=-1).reshape(bp, tiles, 8, f2)
        outs["kp"].append(k_rows.reshape(bp, tp, DIFF_HEADS, -1))
        outs["vp"].append(v_rows.reshape(bp, tp, DIFF_HEADS, -1))
        outs["gp"].append(gla_state)
        outs["mk"].append(mem_k.reshape(bp, mtok, MEM_HEADS, -1))
        outs["mv"].append(mem_v.reshape(bp, mtok, MEM_HEADS, -1))
        outs["cp"].append(tail[:, -1, 6:8])

        hs = rmsnorm_cast(ys, norm_pre_mix[l])
        (qkvr_s,) = matmul(hs, wts["gla"], [F32])
        (ag_s,) = matmul(hs, wts["ag"], [F32], tn=LANES)
        (qd_s,) = matmul(hs, wts["qd"], [F32])
        (kd_s,) = matmul(hs, wts["kd"], [F32])
        (vd_s,) = matmul(hs, wts["vd"], [F32])
        (qmg_s,) = matmul(hs, wts["qm_gates"], [F32])
        a_s, gla_state_s = gla_step(qkvr_s, ag_s, wts["w2"], b_gla_gate[l], gla_norm[l], state_gla[l], dk, dv)
        b_s = diff_attention_paged(qd_s, kd_s, vd_s, lam, cache_k, cache_v, l, page_table, rel_bias_table,
                                   diff_norm[l], lam_init)
        c_s = memory_attention(qmg_s.reshape(bs, ts, -1), 0, cache_mem_k[l].reshape(bs, -1, d),
                               cache_mem_v[l].reshape(bs, -1, d), F32)
        x1_s, h2_s = merge_out_proj(a_s, b_s, c_s.reshape(bs, d), qmg_s, 1, ys, w_out_b, norm_post_mix[l],
                                    norm_pre_ffn[l])
        st = state_ffn_conv[l].astype(F32)
        act_s, ug_s, uv_s = ffn_up_step(h2_s, w_up_b, ffn_conv_w[l], ffn_conv_b[l], jnp.swapaxes(st, 0, 1))
        ys = ffn_down(act_s, w_down_b, x1_s, norm_post_ffn[l])
        u_s = jnp.concatenate([ug_s, uv_s], axis=-1)
        outs["ks"].append(kd_s.reshape(bs, ts, DIFF_HEADS, -1))
        outs["vs"].append(vd_s.reshape(bs, ts, DIFF_HEADS, -1))
        outs["gs"].append(gla_state_s)
        outs["cs"].append(jnp.stack([st[:, 1], u_s], axis=1))

    st = lambda n: jnp.stack(outs[n])
    return (yp.reshape(bp, tp, d), ys.reshape(bs, ts, d), st("kp"), st("vp"), st("gp"), st("mk"), st("mv"),
            st("cp"), st("ks"), st("vs"), st("gs"), st("cs"))
```

```python
import functools
import math

import jax
import jax.numpy as jnp
from jax import lax
from jax.experimental import pallas as pl
from jax.experimental.pallas import tpu as pltpu

F32 = jnp.float32
BF16 = jnp.bfloat16

GLA_HEADS = 4
GLA_GATE_TAU = 16.0
DIFF_HEADS = 8
MEM_HEADS = 4
REL_BUCKETS = 32
REL_MAX_DIST = 128
N_BRANCH = 3
EPS = 1e-6

NEG_BIG = -1e30
LANES = 128
VMEM_LIMIT_V7X = 56 * 1024 * 1024
GLA_BLOCK = 256
GLA_SUB = 16
GLA_MILD_LOG_DECAY = 60.0
ATTN_BLOCK = 512
PAGES_PER_STEP = 8


def _cparams(*sem):
    return pltpu.CompilerParams(dimension_semantics=sem, vmem_limit_bytes=VMEM_LIMIT_V7X)


def _nt_dot(a, b):
    return lax.dot_general(a, b, (((1,), (1,)), ((), ())), preferred_element_type=F32)


def _tn_dot(a, b):
    return lax.dot_general(a, b, (((0,), (0,)), ((), ())), preferred_element_type=F32)


def _rms(x, g):
    return x * lax.rsqrt(jnp.mean(x * x, axis=-1, keepdims=True) + EPS) * g


def _sigmoid(x):
    return 0.5 * jnp.tanh(0.5 * x) + 0.5


def _norm_kernel(x_ref, g_ref, o_ref):
    o_ref[...] = _rms(x_ref[...], g_ref[...]).astype(o_ref.dtype)


def rmsnorm_cast(x, g, tm=512):
    m, d = x.shape
    tm = min(tm, m)
    assert m % tm == 0
    return pl.pallas_call(
        _norm_kernel,
        out_shape=jax.ShapeDtypeStruct((m, d), BF16),
        grid=(m // tm,),
        in_specs=[pl.BlockSpec((tm, d), lambda i: (i, 0)), pl.BlockSpec((1, d), lambda i: (0, 0))],
        out_specs=pl.BlockSpec((tm, d), lambda i: (i, 0)),
        compiler_params=_cparams("parallel"),
        name="rmsnorm_cast",
    )(x, g.reshape(1, d))


def _mm_kernel(x_ref, w_ref, *o_refs):
    acc = jnp.dot(x_ref[...], w_ref[...], preferred_element_type=F32)
    for o_ref in o_refs:
        o_ref[...] = acc.astype(o_ref.dtype)


def matmul(x, w, out_dtypes, cols=None, tm=1024, tn=1024):
    m, k = x.shape
    c0, n = cols if cols is not None else (0, w.shape[1])
    tm, tn = min(tm, m), min(tn, n)
    assert m % tm == 0 and n % tn == 0 and c0 % tn == 0
    jb = c0 // tn
    return pl.pallas_call(
        _mm_kernel,
        out_shape=[jax.ShapeDtypeStruct((m, n), dt) for dt in out_dtypes],
        grid=(m // tm, n // tn),
        in_specs=[pl.BlockSpec((tm, k), lambda i, j: (i, 0)), pl.BlockSpec((k, tn), lambda i, j: (0, jb + j))],
        out_specs=[pl.BlockSpec((tm, tn), lambda i, j: (i, j)) for _ in out_dtypes],
        compiler_params=_cparams("parallel", "parallel"),
        name="matmul",
    )(x, w)


def _log_sigmoid(z):
    return -(jnp.maximum(-z, 0.0) + jnp.log1p(jnp.exp(-jnp.abs(z))))


def _gla_log_decay(ag, w2, gb):
    z = jnp.dot(ag.astype(BF16), w2.astype(BF16), preferred_element_type=F32)
    return _log_sigmoid(z + gb) * (1.0 / GLA_GATE_TAU)


def _gla_prompt_kernel(q_ref, k_ref, v_ref, r_ref, ag_ref, w2_ref, gb_ref, gn_ref, a_ref, s_ref,
                       b_scr, amat_scr, *, tb, dk):
    @pl.when(pl.program_id(2) == 0)
    def _():
        s_ref[...] = jnp.zeros_like(s_ref)

    scale = dk ** -0.5
    q = q_ref[...].astype(F32) * scale
    k = k_ref[...].astype(F32)
    v = v_ref[...]
    g = _gla_log_decay(ag_ref[...], w2_ref[...], gb_ref[...])
    row = lax.broadcasted_iota(jnp.int32, (tb, tb), 0)
    col = lax.broadcasted_iota(jnp.int32, (tb, tb), 1)
    causal = col <= row
    tril = jnp.where(causal, 1.0, 0.0).astype(BF16)
    g_hi = g.astype(BF16)
    g_lo = (g - g_hi.astype(F32)).astype(BF16)
    b = (jnp.dot(tril, g_hi, preferred_element_type=F32) + jnp.dot(tril, g_lo, preferred_element_type=F32))
    b_scr[...] = b
    b_last = b[tb - 1:tb, :]
    k_hat = (k * jnp.exp(b_last - b)).astype(BF16)

    s_prev = s_ref[0, 0]
    o = jnp.dot((q * jnp.exp(b)).astype(BF16), s_prev.astype(BF16), preferred_element_type=F32)

    lane = lax.broadcasted_iota(jnp.int32, (GLA_SUB, tb), 1)
    lane_h = lax.broadcasted_iota(jnp.int32, (GLA_SUB, LANES), 1)
    row_h = lax.broadcasted_iota(jnp.int32, (GLA_SUB, LANES), 0)
    mild = jnp.min(b_last) >= -GLA_MILD_LOG_DECAY

    @pl.when(mild)
    def _():
        q_end = (q * jnp.exp(b - b_last)).astype(BF16)
        amat_scr[...] = jnp.where(causal, _nt_dot(q_end, k_hat), 0.0)

    def sub_chunk(i, carry):
        base = pl.multiple_of(i * GLA_SUB, GLA_SUB)
        q_i = q_ref[pl.ds(base, GLA_SUB), :].astype(F32) * scale
        k_i = k_ref[pl.ds(base, GLA_SUB), :].astype(F32)
        b_i = b_scr[pl.ds(base, GLA_SUB), :]
        r_i = b_i[0:1, :]
        q_t = (q_i * jnp.exp(b_i - r_i)).astype(BF16)
        k_t = (k_ref[...].astype(F32) * jnp.exp(jnp.minimum(r_i - b_scr[...], 0.0))).astype(BF16)
        p = jnp.where(lane < base, _nt_dot(q_t, k_t), 0.0)
        lane0 = base % LANES
        d = jnp.zeros((GLA_SUB, LANES), F32)
        for s in range(GLA_SUB):
            e = jnp.exp(jnp.minimum(b_i - b_i[s:s + 1, :], 0.0))
            c = jnp.sum(q_i * e * k_i[s:s + 1, :], axis=-1, keepdims=True)
            d = d + jnp.where(lane_h == lane0 + s, jnp.where(row_h >= s, c, 0.0), 0.0)
        halves = [jnp.where(base // LANES == h, d, 0.0) for h in range(tb // LANES)]
        amat_scr[pl.ds(base, GLA_SUB), :] = p + jnp.concatenate(halves, axis=1)
        return carry

    @pl.when(jnp.logical_not(mild))
    def _():
        lax.fori_loop(0, tb // GLA_SUB, sub_chunk, 0)

    o = o + jnp.dot(amat_scr[...].astype(BF16), v, preferred_element_type=F32)

    decay_col = jnp.broadcast_to(jnp.exp(b_last), (LANES, dk)).T
    dv = v.shape[1]
    decay_full = jnp.concatenate([decay_col] * (dv // LANES), axis=1)
    s_ref[0, 0] = s_prev * decay_full + _tn_dot(k_hat, v)

    rgate = r_ref[...].astype(F32)
    a_ref[...] = (_rms(o, gn_ref[...]) * (rgate * _sigmoid(rgate))).astype(a_ref.dtype)


def gla_prompt(qkvr, ag, w2, gb, gnorm, batch, seq, dk, dv):
    heads = GLA_HEADS
    tb = min(GLA_BLOCK, seq)
    nblk = seq // tb
    row = lambda b, h, i: b * nblk + i
    kern = functools.partial(_gla_prompt_kernel, tb=tb, dk=dk)
    return pl.pallas_call(
        kern,
        out_shape=[jax.ShapeDtypeStruct((batch * seq, heads * dv), BF16),
                   jax.ShapeDtypeStruct((batch, heads, dk, dv), F32)],
        grid=(batch, heads, nblk),
        in_specs=[
            pl.BlockSpec((tb, dk), lambda b, h, i: (row(b, h, i), h)),
            pl.BlockSpec((tb, dk), lambda b, h, i: (row(b, h, i), heads + h)),
            pl.BlockSpec((tb, dv), lambda b, h, i: (row(b, h, i), (2 * heads * dk) // dv + h)),
            pl.BlockSpec((tb, dv), lambda b, h, i: (row(b, h, i), (2 * heads * dk) // dv + heads + h)),
            pl.BlockSpec((tb, LANES), lambda b, h, i: (row(b, h, i), 0)),
            pl.BlockSpec((LANES, dk), lambda b, h, i: (0, h)),
            pl.BlockSpec((1, dk), lambda b, h, i: (0, h)),
            pl.BlockSpec((1, dv), lambda b, h, i: (0, 0)),
        ],
        out_specs=[pl.BlockSpec((tb, dv), lambda b, h, i: (row(b, h, i), h)),
                   pl.BlockSpec((1, 1, dk, dv), lambda b, h, i: (b, h, 0, 0))],
        scratch_shapes=[pltpu.VMEM((tb, dk), F32), pltpu.VMEM((tb, tb), F32)],
        compiler_params=_cparams("parallel", "parallel", "arbitrary"),
        name="gla_prompt",
    )(qkvr, qkvr, qkvr, qkvr, ag, w2, gb.reshape(1, -1), gnorm.reshape(1, -1))


def _gla_step_kernel(q_ref, k_ref, v_ref, r_ref, ag_ref, w2_ref, gb_ref, gn_ref, s0_ref, a_ref, s_ref, *, nb, dk):
    g = _gla_log_decay(ag_ref[...], w2_ref[...], gb_ref[...])
    reps = LANES // nb
    decay = jnp.exp(g)
    decay_t = jnp.concatenate([decay] * reps, axis=0).T
    k = k_ref[...]
    k_t = jnp.concatenate([k] * reps, axis=0).T
    q = q_ref[...] * (dk ** -0.5)
    q_dec = (q * decay).astype(BF16)
    v = v_ref[...]
    rows = lax.broadcasted_iota(jnp.int32, (nb, v.shape[1]), 0)
    o = jnp.sum(q * k, axis=-1, keepdims=True) * v
    for j in range(nb):
        s0 = s0_ref[j, 0]
        s_ref[j, 0] = s0 * decay_t[:, j:j + 1] + k_t[:, j:j + 1] * v[j:j + 1, :]
        o = o + jnp.where(rows == j, jnp.dot(q_dec, s0.astype(BF16), preferred_element_type=F32), 0.0)
    rgate = r_ref[...]
    a_ref[...] = _rms(o, gn_ref[...]) * (rgate * _sigmoid(rgate))


def gla_step(proj, ag_blk, w2, gb, gnorm, s0, dk, dv):
    heads = GLA_HEADS
    batch = proj.shape[0]
    nb = 8
    kern = functools.partial(_gla_step_kernel, nb=nb, dk=dk)
    return pl.pallas_call(
        kern,
        out_shape=[jax.ShapeDtypeStruct((batch, heads * dv), F32),
                   jax.ShapeDtypeStruct((batch, heads, dk, dv), F32)],
        grid=(batch // nb, heads),
        in_specs=[
            pl.BlockSpec((nb, dk), lambda b, h: (b, h)),
            pl.BlockSpec((nb, dk), lambda b, h: (b, heads + h)),
            pl.BlockSpec((nb, dv), lambda b, h: (b, (2 * heads * dk) // dv + h)),
            pl.BlockSpec((nb, dv), lambda b, h: (b, (2 * heads * dk) // dv + heads + h)),
            pl.BlockSpec((nb, LANES), lambda b, h: (b, ag_blk)),
            pl.BlockSpec((LANES, dk), lambda b, h: (0, h)),
            pl.BlockSpec((1, dk), lambda b, h: (0, h)),
            pl.BlockSpec((1, dv), lambda b, h: (0, 0)),
            pl.BlockSpec((nb, 1, dk, dv), lambda b, h: (b, h, 0, 0)),
        ],
        out_specs=[pl.BlockSpec((nb, dv), lambda b, h: (b, h)),
                   pl.BlockSpec((nb, 1, dk, dv), lambda b, h: (b, h, 0, 0))],
        compiler_params=_cparams("parallel", "parallel"),
        name="gla_step",
    )(proj, proj, proj, proj, proj, w2, gb.reshape(1, -1), gnorm.reshape(1, -1), s0)


def _rel_bucket(dist):
    n = jnp.maximum(dist, 0)
    max_exact = REL_BUCKETS // 2
    large = max_exact + (jnp.log(jnp.maximum(n, 1).astype(F32) / max_exact)
                         / math.log(REL_MAX_DIST / max_exact) * (REL_BUCKETS - max_exact)).astype(jnp.int32)
    large = jnp.minimum(large, REL_BUCKETS - 1)
    return jnp.where(n < max_exact, n, large)


def _lane_tile(x, width):
    return jnp.concatenate([x] * (width // LANES), axis=1)


def _diff_prompt_kernel(lam_ref, q_ref, k_ref, v_ref, w_ref, far_ref, dn_ref, o_ref,
                        m_scr, l_scr, acc_scr, bias_scr, *, tq, dh, out_scale):
    qi = pl.program_id(2)
    h = pl.program_id(1)
    scale = dh ** -0.5
    far = far_ref[h]

    @pl.when(qi == 0)
    def _():
        toep = pltpu.roll(jnp.broadcast_to(w_ref[0], (tq, tq)), 0, 1, stride=1, stride_axis=0)
        row = lax.broadcasted_iota(jnp.int32, (tq, tq), 0)
        col = lax.broadcasted_iota(jnp.int32, (tq, tq), 1)
        bias_scr[0] = jnp.where(col <= row, toep, NEG_BIG)
        bias_scr[1] = jnp.where(col > row, toep, far)

    m_scr[...] = jnp.full_like(m_scr, NEG_BIG)
    l_scr[...] = jnp.zeros_like(l_scr)
    acc_scr[...] = jnp.zeros_like(acc_scr)
    q = q_ref[...]

    def visit(kb, bias):
        start = pl.multiple_of(kb * tq, tq)
        kk = k_ref[pl.ds(start, tq), :]
        vv = v_ref[pl.ds(start, tq), :]
        for c in range(2):
            s = _nt_dot(q[:, c * dh:(c + 1) * dh], kk[:, c * dh:(c + 1) * dh]) * scale + bias
            m_old = m_scr[c]
            m_new = jnp.maximum(m_old, jnp.max(s, axis=-1, keepdims=True))
            corr = jnp.exp(m_old - m_new)
            p = jnp.exp(s - _lane_tile(m_new, tq))
            l_scr[c] = l_scr[c] * corr + jnp.sum(p, axis=-1, keepdims=True)
            acc_scr[c] = (acc_scr[c] * _lane_tile(corr, 2 * dh)
                          + jnp.dot(p.astype(BF16), vv, preferred_element_type=F32))
            m_scr[c] = m_new

    def far_block(kb, carry):
        visit(kb, far)
        return carry

    lax.fori_loop(0, jnp.maximum(qi - 1, 0), far_block, 0)

    @pl.when(qi >= 1)
    def _():
        visit(qi - 1, bias_scr[1])

    visit(qi, bias_scr[0])
    lam = lam_ref[0]
    o = (acc_scr[0] / _lane_tile(l_scr[0], 2 * dh) - lam * (acc_scr[1] / _lane_tile(l_scr[1], 2 * dh)))
    o_ref[...] = (_rms(o, dn_ref[...]) * out_scale).astype(o_ref.dtype)


def diff_attention_prompt(q, k, v, lam, rel_table, diff_norm, batch, seq, lam_init):
    heads = DIFF_HEADS
    dh = q.shape[1] // (2 * heads)
    tq = min(ATTN_BLOCK, seq)
    assert tq >= REL_MAX_DIST or seq == tq
    nq = seq // tq
    dist = (tq - jnp.arange(tq, dtype=jnp.int32)) % tq
    w = rel_table[_rel_bucket(dist)].astype(F32).T.reshape(heads, 1, tq)
    far = rel_table[REL_BUCKETS - 1].astype(F32)
    kern = functools.partial(_diff_prompt_kernel, tq=tq, dh=dh, out_scale=1.0 - lam_init)
    return pl.pallas_call(
        kern,
        out_shape=jax.ShapeDtypeStruct((batch * seq, heads * 2 * dh), BF16),
        grid=(batch, heads, nq),
        in_specs=[
            pl.BlockSpec(memory_space=pltpu.SMEM),
            pl.BlockSpec((tq, 2 * dh), lambda b, h, i: (b * nq + i, h)),
            pl.BlockSpec((seq, 2 * dh), lambda b, h, i: (b, h)),
            pl.BlockSpec((seq, 2 * dh), lambda b, h, i: (b, h)),
            pl.BlockSpec((1, 1, tq), lambda b, h, i: (h, 0, 0)),
            pl.BlockSpec(memory_space=pltpu.SMEM),
            pl.BlockSpec((1, 2 * dh), lambda b, h, i: (0, 0)),
        ],
        out_specs=pl.BlockSpec((tq, 2 * dh), lambda b, h, i: (b * nq + i, h)),
        scratch_shapes=[pltpu.VMEM((2, tq, LANES), F32), pltpu.VMEM((2, tq, LANES), F32),
                        pltpu.VMEM((2, tq, 2 * dh), F32), pltpu.VMEM((2, tq, tq), F32)],
        compiler_params=_cparams("arbitrary", "arbitrary", "arbitrary"),
        name="diff_attention_prompt",
    )(lam.reshape(1), q, k, v, w, far, diff_norm.reshape(1, -1))


def _diff_paged_kernel(pt_ref, lam_ref, q_ref, *rest, n_pages, pages, out_scale, heads):
    k_refs = rest[:pages]
    v_refs = rest[pages:2 * pages]
    bias_ref, kn_ref, vn_ref, bn_ref, dn_ref, o_ref, m_scr, l_scr, acc_scr = rest[2 * pages:]
    del pt_ref
    jg = pl.program_id(1)

    @pl.when(jg == 0)
    def _():
        m_scr[...] = jnp.full_like(m_scr, NEG_BIG)
        l_scr[...] = jnp.zeros_like(l_scr)
        acc_scr[...] = jnp.zeros_like(acc_scr)

    q = q_ref[0]

    def update(ks, vs, biases):
        ss = [_nt_dot(q, kk.astype(BF16)) + bias for kk, bias in zip(ks, biases)]
        m_old = m_scr[...]
        m_new = m_old
        for s in ss:
            m_new = jnp.maximum(m_new, jnp.max(s, axis=-1, keepdims=True))
        corr = jnp.exp(m_old - m_new)
        l_new = l_scr[...] * corr
        acc = acc_scr[...] * corr
        for s, vv in zip(ss, vs):
            p = jnp.exp(s - m_new)
            l_new = l_new + jnp.sum(p, axis=-1, keepdims=True)
            acc = acc + jnp.dot(p.astype(BF16), vv.astype(BF16), preferred_element_type=F32)
        l_scr[...] = l_new
        acc_scr[...] = acc
        m_scr[...] = m_new

    def as_rows(ref):
        x = ref[0, 0]
        return x.reshape(x.shape[0] * x.shape[1], x.shape[2])

    biases = [bias_ref[(jg * pages + i == n_pages - 1).astype(jnp.int32)] for i in range(pages)]
    update([as_rows(r) for r in k_refs], [as_rows(r) for r in v_refs], biases)

    @pl.when(jg == pl.num_programs(1) - 1)
    def _():
        update([kn_ref[0]], [vn_ref[0]], [bn_ref[...]])
        o = acc_scr[...] / l_scr[...]
        out = o[:heads] - lam_ref[0] * o[heads:]
        o_ref[0] = _rms(out, dn_ref[...]) * out_scale


def diff_attention_paged(q, k_new, v_new, lam, cache_k, cache_v, layer, page_table, rel_table, diff_norm, lam_init):
    heads = DIFF_HEADS
    batch = q.shape[0]
    dh = q.shape[1] // (2 * heads)
    n_pages = page_table.shape[1]
    page = cache_k.shape[2]
    past = n_pages * page
    pages = math.gcd(PAGES_PER_STEP, n_pages)
    assert page >= REL_MAX_DIST
    rows = page * heads
    q4 = (q.reshape(batch, heads, 2, dh) * (dh ** -0.5)).astype(BF16)
    zeros = jnp.zeros_like(q4[:, :, 0])
    q_mat = jnp.concatenate([jnp.concatenate([q4[:, :, 0], zeros], axis=-1),
                             jnp.concatenate([zeros, q4[:, :, 1]], axis=-1)], axis=1)
    tok = jnp.arange(page, dtype=jnp.int32)
    dist_last = past - ((n_pages - 1) * page + tok)
    b_far = jnp.broadcast_to(rel_table[REL_BUCKETS - 1].astype(F32)[:, None], (heads, page))
    b_last = rel_table[_rel_bucket(dist_last)].astype(F32).T
    own = (jnp.arange(heads)[:, None, None] == jnp.arange(heads)[None, None, :])
    def expand(bh):
        t = jnp.where(own, bh[:, :, None], NEG_BIG).reshape(heads, rows)
        return jnp.concatenate([t, t], axis=0)
    bias = jnp.stack([expand(b_far), expand(b_last)])
    pad = LANES - heads
    kn = jnp.pad(k_new.reshape(batch, heads, 2 * dh), ((0, 0), (0, pad), (0, 0)))
    vn = jnp.pad(v_new.reshape(batch, heads, 2 * dh), ((0, 0), (0, pad), (0, 0)))
    b0 = jnp.where(jnp.arange(heads)[:, None] == jnp.arange(LANES)[None, :],
                   rel_table[0].astype(F32)[:, None], NEG_BIG)
    bn = jnp.concatenate([b0, b0], axis=0)

    def page_spec(i):
        return pl.BlockSpec((1, 1) + cache_k.shape[2:],
                            lambda b, j, pt: (layer, pt[b * n_pages + j * pages + i], 0, 0, 0))

    kern = functools.partial(_diff_paged_kernel, n_pages=n_pages, pages=pages, out_scale=1.0 - lam_init, heads=heads)
    out = pl.pallas_call(
        kern,
        out_shape=jax.ShapeDtypeStruct((batch, heads, 2 * dh), F32),
        grid_spec=pltpu.PrefetchScalarGridSpec(
            num_scalar_prefetch=1,
            grid=(batch, n_pages // pages),
            in_specs=[pl.BlockSpec(memory_space=pltpu.SMEM),
                      pl.BlockSpec((1, 2 * heads, 2 * dh), lambda b, j, pt: (b, 0, 0))]
                     + [page_spec(i) for i in range(pages)] + [page_spec(i) for i in range(pages)]
                     + [pl.BlockSpec((2, 2 * heads, rows), lambda b, j, pt: (0, 0, 0)),
                        pl.BlockSpec((1, LANES, 2 * dh), lambda b, j, pt: (b, 0, 0)),
                        pl.BlockSpec((1, LANES, 2 * dh), lambda b, j, pt: (b, 0, 0)),
                        pl.BlockSpec((2 * heads, LANES), lambda b, j, pt: (0, 0)),
                        pl.BlockSpec((1, 2 * dh), lambda b, j, pt: (0, 0))],
            out_specs=pl.BlockSpec((1, heads, 2 * dh), lambda b, j, pt: (b, 0, 0)),
            scratch_shapes=[pltpu.VMEM((2 * heads, 1), F32), pltpu.VMEM((2 * heads, 1), F32),
                            pltpu.VMEM((2 * heads, 2 * dh), F32)],
        ),
        compiler_params=_cparams("parallel", "arbitrary"),
        name="diff_attention_paged",
    )(page_table.reshape(-1), lam.reshape(1), q_mat, *([cache_k] * pages), *([cache_v] * pages),
      bias, kn, vn, bn, diff_norm.reshape(1, -1))
    return out.reshape(batch, heads * 2 * dh)


def _mem_attn_kernel(q_ref, k_ref, v_ref, o_ref, *, scale):
    q = q_ref[0]
    tq = q.shape[0]
    if tq < 8:
        q = jnp.broadcast_to(q[0:1], (8, q.shape[1]))
    s = _nt_dot(q.astype(BF16), k_ref[0].astype(BF16)) * scale
    p = jnp.exp(s - jnp.max(s, axis=-1, keepdims=True))
    o = jnp.dot(p.astype(BF16), v_ref[0].astype(BF16), preferred_element_type=F32)
    o = o / jnp.sum(p, axis=-1, keepdims=True)
    o_ref[0] = o[:tq].astype(o_ref.dtype)


def memory_attention(q3, col0, mem_k, mem_v, out_dtype, tq=512):
    batch, seq, _ = q3.shape
    mtok, width = mem_k.shape[1:]
    md = width // MEM_HEADS
    tq = min(tq, seq)
    kern = functools.partial(_mem_attn_kernel, scale=md ** -0.5)
    return pl.pallas_call(
        kern,
        out_shape=jax.ShapeDtypeStruct((batch, seq, width), out_dtype),
        grid=(batch, MEM_HEADS, seq // tq),
        in_specs=[pl.BlockSpec((1, tq, md), lambda b, h, i: (b, i, col0 + h)),
                  pl.BlockSpec((1, mtok, md), lambda b, h, i: (b, 0, h)),
                  pl.BlockSpec((1, mtok, md), lambda b, h, i: (b, 0, h))],
        out_specs=pl.BlockSpec((1, tq, md), lambda b, h, i: (b, i, h)),
        compiler_params=_cparams("parallel", "parallel", "parallel"),
        name="memory_attention",
    )(q3, mem_k, mem_v)


def _merge_kernel(a_ref, b_ref, c_ref, g0_ref, g1_ref, g2_ref, x_ref, w_ref, npost_ref, npre_ref, x1_ref, h2_ref):
    merged = (_sigmoid(g0_ref[...].astype(F32)) * a_ref[...].astype(F32)
              + _sigmoid(g1_ref[...].astype(F32)) * b_ref[...].astype(F32)
              + _sigmoid(g2_ref[...].astype(F32)) * c_ref[...].astype(F32))
    y = jnp.dot(merged.astype(BF16), w_ref[...], preferred_element_type=F32)
    x1 = x_ref[...] + _rms(y, npost_ref[...])
    x1_ref[...] = x1
    h2_ref[...] = _rms(x1, npre_ref[...]).astype(h2_ref.dtype)


def merge_out_proj(a, b, c, gates_arr, gate_col0, x, w_out, norm_post, norm_pre_ffn, tm=256):
    m, d = x.shape
    tm = min(tm, m)
    rowblk = lambda i: (i, 0)
    vec = pl.BlockSpec((1, d), lambda i: (0, 0))
    return pl.pallas_call(
        _merge_kernel,
        out_shape=[jax.ShapeDtypeStruct((m, d), F32), jax.ShapeDtypeStruct((m, d), BF16)],
        grid=(m // tm,),
        in_specs=[pl.BlockSpec((tm, d), rowblk), pl.BlockSpec((tm, d), rowblk), pl.BlockSpec((tm, d), rowblk),
                  pl.BlockSpec((tm, d), lambda i: (i, gate_col0)),
                  pl.BlockSpec((tm, d), lambda i: (i, gate_col0 + 1)),
                  pl.BlockSpec((tm, d), lambda i: (i, gate_col0 + 2)),
                  pl.BlockSpec((tm, d), rowblk),
                  pl.BlockSpec((d, d), lambda i: (0, 0)),
                  vec, vec],
        out_specs=[pl.BlockSpec((tm, d), rowblk), pl.BlockSpec((tm, d), rowblk)],
        compiler_params=_cparams("parallel"),
        name="merge_out_proj",
    )(a, b, c, gates_arr, gates_arr, gates_arr, x, w_out, norm_post.reshape(1, d), norm_pre_ffn.reshape(1, d))


def _gelu_tanh(x):
    return 0.5 * x * (1.0 + jnp.tanh(math.sqrt(2.0 / math.pi) * (x + 0.044715 * (x * x * x))))


def _ffn_up_seq_kernel(h_ref, wg_ref, wv_ref, cwg_ref, cwv_ref, cbg_ref, cbv_ref, stg_ref, stv_ref,
                       act_ref, tailg_ref, tailv_ref, carryg, carryv, *, tiles_per_seq):
    mi = pl.program_id(1)
    h = h_ref[...]
    tm = h.shape[0]
    row = lax.broadcasted_iota(jnp.int32, (tm, 1), 0)
    seq_start = mi % tiles_per_seq == 0

    def conv(w_ref, cw_ref, cb_ref, st_ref, carry, tail_ref):
        u = jnp.dot(h, w_ref[...], preferred_element_type=F32)
        st = st_ref[0]
        prev2 = jnp.where(seq_start, st[0:1, :], carry[6:7, :])
        prev1 = jnp.where(seq_start, st[1:2, :], carry[7:8, :])
        u1 = jnp.where(row >= 1, pltpu.roll(u, 1, 0), prev1)
        u2 = jnp.where(row >= 2, pltpu.roll(u, 2, 0), jnp.where(row == 1, prev1, prev2))
        cw = cw_ref[...]
        tail = u[tm - 8:tm, :]
        carry[...] = tail
        tail_ref[0] = tail
        return cb_ref[...] + u2 * cw[0:1, :] + u1 * cw[1:2, :] + u * cw[2:3, :]

    gate = conv(wg_ref, cwg_ref, cbg_ref, stg_ref, carryg, tailg_ref)
    val = conv(wv_ref, cwv_ref, cbv_ref, stv_ref, carryv, tailv_ref)
    act_ref[...] = (_gelu_tanh(gate) * val).astype(act_ref.dtype)


def ffn_up_seq(h2, w_up, conv_w, conv_b, conv_state, seq, tm=1024, tn=512):
    m, d = h2.shape
    f = w_up.shape[1] // 2
    tm = min(tm, seq)
    tiles_per_seq = seq // tm
    nj = f // tn
    kern = functools.partial(_ffn_up_seq_kernel, tiles_per_seq=tiles_per_seq)
    cw = conv_w.astype(F32)
    cb = conv_b.reshape(1, -1).astype(F32)
    return pl.pallas_call(
        kern,
        out_shape=[jax.ShapeDtypeStruct((m, f), BF16),
                   jax.ShapeDtypeStruct((m // tm, 8, f), F32),
                   jax.ShapeDtypeStruct((m // tm, 8, f), F32)],
        grid=(nj, m // tm),
        in_specs=[pl.BlockSpec((tm, d), lambda j, i: (i, 0)),
                  pl.BlockSpec((d, tn), lambda j, i: (0, j)),
                  pl.BlockSpec((d, tn), lambda j, i: (0, nj + j)),
                  pl.BlockSpec((cw.shape[0], tn), lambda j, i: (0, j)),
                  pl.BlockSpec((cw.shape[0], tn), lambda j, i: (0, nj + j)),
                  pl.BlockSpec((1, tn), lambda j, i: (0, j)),
                  pl.BlockSpec((1, tn), lambda j, i: (0, nj + j)),
                  pl.BlockSpec((1, 2, tn), lambda j, i: (i // tiles_per_seq, 0, j)),
                  pl.BlockSpec((1, 2, tn), lambda j, i: (i // tiles_per_seq, 0, nj + j))],
        out_specs=[pl.BlockSpec((tm, tn), lambda j, i: (i, j)),
                   pl.BlockSpec((1, 8, tn), lambda j, i: (i, 0, j)),
                   pl.BlockSpec((1, 8, tn), lambda j, i: (i, 0, j))],
        scratch_shapes=[pltpu.VMEM((8, tn), F32), pltpu.VMEM((8, tn), F32)],
        compiler_params=_cparams("parallel", "arbitrary"),
        name="ffn_up_seq",
    )(h2, w_up, w_up, cw, cw, cb, cb, conv_state, conv_state)


def _ffn_up_step_kernel(h_ref, wg_ref, wv_ref, cwg_ref, cwv_ref, cbg_ref, cbv_ref, stg_ref, stv_ref,
                        act_ref, ug_ref, uv_ref):
    h = h_ref[...]

    def conv(w_ref, cw_ref, cb_ref, st_ref, u_ref):
        u = jnp.dot(h, w_ref[...], preferred_element_type=F32)
        u_ref[...] = u
        cw = cw_ref[...]
        return cb_ref[...] + st_ref[0] * cw[0:1, :] + st_ref[1] * cw[1:2, :] + u * cw[2:3, :]

    gate = conv(wg_ref, cwg_ref, cbg_ref, stg_ref, ug_ref)
    val = conv(wv_ref, cwv_ref, cbv_ref, stv_ref, uv_ref)
    act_ref[...] = (_gelu_tanh(gate) * val).astype(act_ref.dtype)


def ffn_up_step(h2, w_up, conv_w, conv_b, conv_state_t, tn=512):
    m, d = h2.shape
    f = w_up.shape[1] // 2
    nj = f // tn
    cw = conv_w.astype(F32)
    cb = conv_b.reshape(1, -1).astype(F32)
    return pl.pallas_call(
        _ffn_up_step_kernel,
        out_shape=[jax.ShapeDtypeStruct((m, f), BF16), jax.ShapeDtypeStruct((m, f), F32),
                   jax.ShapeDtypeStruct((m, f), F32)],
        grid=(nj,),
        in_specs=[pl.BlockSpec((m, d), lambda j: (0, 0)),
                  pl.BlockSpec((d, tn), lambda j: (0, j)),
                  pl.BlockSpec((d, tn), lambda j: (0, nj + j)),
                  pl.BlockSpec((cw.shape[0], tn), lambda j: (0, j)),
                  pl.BlockSpec((cw.shape[0], tn), lambda j: (0, nj + j)),
                  pl.BlockSpec((1, tn), lambda j: (0, j)),
                  pl.BlockSpec((1, tn), lambda j: (0, nj + j)),
                  pl.BlockSpec((2, m, tn), lambda j: (0, 0, j)),
                  pl.BlockSpec((2, m, tn), lambda j: (0, 0, nj + j))],
        out_specs=[pl.BlockSpec((m, tn), lambda j: (0, j)),
                   pl.BlockSpec((m, tn), lambda j: (0, j)),
                   pl.BlockSpec((m, tn), lambda j: (0, j))],
        compiler_params=_cparams("parallel"),
        name="ffn_up_step",
    )(h2, w_up, w_up, cw, cw, cb, cb, conv_state_t, conv_state_t)


def _ffn_down_kernel(act_ref, w_ref, x_ref, n_ref, y_ref):
    f = jnp.dot(act_ref[...], w_ref[...], preferred_element_type=F32)
    y_ref[...] = x_ref[...] + _rms(f, n_ref[...])


def ffn_down(act, w_down, x1, norm_post, tm=512):
    m, f = act.shape
    d = w_down.shape[1]
    tm = min(tm, m)
    assert m % tm == 0
    return pl.pallas_call(
        _ffn_down_kernel,
        out_shape=jax.ShapeDtypeStruct((m, d), F32),
        grid=(m // tm,),
        in_specs=[pl.BlockSpec((tm, f), lambda i: (i, 0)),
                  pl.BlockSpec((f, d), lambda i: (0, 0), pipeline_mode=pl.Buffered(1)),
                  pl.BlockSpec((tm, d), lambda i: (i, 0)),
                  pl.BlockSpec((1, d), lambda i: (0, 0))],
        out_specs=pl.BlockSpec((tm, d), lambda i: (i, 0)),
        compiler_params=_cparams("parallel"),
        name="ffn_down",
    )(act, w_down, x1, norm_post.reshape(1, d))


def _aligned_in_proj(w_in, rank):
    d = w_in.shape[0]
    o_ag = 3 * d
    return jnp.concatenate([w_in[:, :o_ag], w_in[:, o_ag + rank:], w_in[:, o_ag:o_ag + rank],
                            jnp.zeros((d, LANES - rank), w_in.dtype)], axis=1).astype(BF16)


def kernel(x_prompt, x_sample, cache_k, cache_v, state_gla, cache_mem_k, cache_mem_v, state_ffn_conv, page_table, mem_prompt, rel_bias_table, norm_pre_mix, norm_post_mix, norm_pre_ffn, norm_post_ffn, norm_mem, w_in, w_gla_gate2, b_gla_gate, gla_norm, diff_lambda_q1, diff_lambda_k1, diff_lambda_q2, diff_lambda_k2, diff_norm, w_mem_kv, w_out, w_up, ffn_conv_w, ffn_conv_b, w_down):
    depth = w_in.shape[0]
    bp, tp, d = x_prompt.shape
    bs, ts, _ = x_sample.shape
    assert ts == 1
    dk, dv = d // (2 * GLA_HEADS), d // GLA_HEADS
    mtok = mem_prompt.shape[1]
    f2 = w_up.shape[2]

    yp = x_prompt.reshape(bp * tp, d)
    ys = x_sample.reshape(bs * ts, d)
    outs = {n: [] for n in ("kp", "vp", "gp", "mk", "mv", "cp", "ks", "vs", "gs", "cs")}
    for l in range(depth):
        lam_init = 0.8 - 0.6 * math.exp(-0.3 * l)
        lam = (jnp.exp(jnp.sum(diff_lambda_q1[l].astype(F32) * diff_lambda_k1[l].astype(F32)))
               - jnp.exp(jnp.sum(diff_lambda_q2[l].astype(F32) * diff_lambda_k2[l].astype(F32))) + lam_init)
        rank = w_gla_gate2.shape[1]
        w_all = _aligned_in_proj(w_in[l], rank)
        w2 = jnp.pad(w_gla_gate2[l].astype(F32), ((0, LANES - rank), (0, 0)))
        c_qd, c_kd, c_vd, c_qm, c_ag = 3 * d, 4 * d, 5 * d, 6 * d, 10 * d
        w_out_b, w_up_b, w_down_b = w_out[l].astype(BF16), w_up[l].astype(BF16), w_down[l].astype(BF16)
        w_mem_b = w_mem_kv[l].astype(BF16)

        hm = rmsnorm_cast(mem_prompt.reshape(bp * mtok, d), norm_mem[l])
        (mem_k,) = matmul(hm, w_mem_b, [F32], cols=(0, d))
        (mem_v,) = matmul(hm, w_mem_b, [F32], cols=(d, d))
        h = rmsnorm_cast(yp, norm_pre_mix[l])
        (qkvr,) = matmul(h, w_all, [BF16], cols=(0, 3 * d))
        (ag,) = matmul(h, w_all, [F32], cols=(c_ag, LANES))
        (qd,) = matmul(h, w_all, [BF16], cols=(c_qd, d))
        k_rows, kd_b = matmul(h, w_all, [F32, BF16], cols=(c_kd, d))
        v_rows, vd_b = matmul(h, w_all, [F32, BF16], cols=(c_vd, d))
        (qm_gates,) = matmul(h, w_all, [BF16], cols=(c_qm, 4 * d))
        a, gla_state = gla_prompt(qkvr, ag, w2, b_gla_gate[l], gla_norm[l], bp, tp, dk, dv)
        b = diff_attention_prompt(qd, kd_b, vd_b, lam, rel_bias_table, diff_norm[l], bp, tp, lam_init)
        c = memory_attention(qm_gates.reshape(bp, tp, -1), 0, mem_k.reshape(bp, mtok, d),
                             mem_v.reshape(bp, mtok, d), BF16)
        x1, h2 = merge_out_proj(a, b, c.reshape(bp * tp, d), qm_gates, 1, yp, w_out_b, norm_post_mix[l],
                                norm_pre_ffn[l])
        conv0 = jnp.zeros((bp, ffn_conv_w.shape[1] - 1, f2), F32)
        act, tail_g, tail_v = ffn_up_seq(h2, w_up_b, ffn_conv_w[l], ffn_conv_b[l], conv0, tp)
        yp = ffn_down(act, w_down_b, x1, norm_post_ffn[l])
        tiles = tail_g.shape[0] // bp
        tail = jnp.concatenate([tail_g, tail_v], axis=-1).reshape(bp, tiles, 8, f2)
        outs["kp"].append(k_rows.reshape(bp, tp, DIFF_HEADS, -1))
        outs["vp"].append(v_rows.reshape(bp, tp, DIFF_HEADS, -1))
        outs["gp"].append(gla_state)
        outs["mk"].append(mem_k.reshape(bp, mtok, MEM_HEADS, -1))
        outs["mv"].append(mem_v.reshape(bp, mtok, MEM_HEADS, -1))
        outs["cp"].append(tail[:, -1, 6:8])

        hs = rmsnorm_cast(ys, norm_pre_mix[l])
        n_lane_tiles = w_all.shape[1] // LANES
        tn_s = LANES * max(t for t in range(1, 9) if n_lane_tiles % t == 0)
        (proj_s,) = matmul(hs, w_all, [F32], tn=tn_s)
        kd_s, vd_s = proj_s[:, c_kd:c_kd + d], proj_s[:, c_vd:c_vd + d]
        a_s, gla_state_s = gla_step(proj_s, c_ag // LANES, w2, b_gla_gate[l], gla_norm[l], state_gla[l], dk, dv)
        b_s = diff_attention_paged(proj_s[:, c_qd:c_qd + d], kd_s, vd_s, lam, cache_k, cache_v, l, page_table,
                                   rel_bias_table, diff_norm[l], lam_init)
        c_s = memory_attention(proj_s.reshape(bs, ts, -1), c_qm // (d // MEM_HEADS),
                               cache_mem_k[l].reshape(bs, -1, d), cache_mem_v[l].reshape(bs, -1, d), F32)
        x1_s, h2_s = merge_out_proj(a_s, b_s, c_s.reshape(bs, d), proj_s, c_qm // d + 1, ys, w_out_b,
                                    norm_post_mix[l], norm_pre_ffn[l])
        st = state_ffn_conv[l].astype(F32)
        act_s, ug_s, uv_s = ffn_up_step(h2_s, w_up_b, ffn_conv_w[l], ffn_conv_b[l], jnp.swapaxes(st, 0, 1))
        ys = ffn_down(act_s, w_down_b, x1_s, norm_post_ffn[l])
        u_s = jnp.concatenate([ug_s, uv_s], axis=-1)
        outs["ks"].append(kd_s.reshape(bs, ts, DIFF_HEADS, -1))
        outs["vs"].append(vd_s.reshape(bs, ts, DIFF_HEADS, -1))
        outs["gs"].append(gla_state_s)
        outs["cs"].append(jnp.stack([st[:, 1], u_s], axis=1))

    st = lambda n: jnp.stack(outs[n])
    return (yp.reshape(bp, tp, d), ys.reshape(bs, ts, d), st("kp"), st("vp"), st("gp"), st("mk"), st("mv"),
            st("cp"), st("ks"), st("vs"), st("gs"), st("cs"))
```

```python
import functools
import math

import jax
import jax.numpy as jnp
from jax import lax
from jax.experimental import pallas as pl
from jax.experimental.pallas import tpu as pltpu

F32 = jnp.float32
BF16 = jnp.bfloat16

GLA_HEADS = 4
GLA_GATE_TAU = 16.0
DIFF_HEADS = 8
MEM_HEADS = 4
REL_BUCKETS = 32
REL_MAX_DIST = 128
N_BRANCH = 3
EPS = 1e-6

NEG_BIG = -1e30
LOG2E = math.log2(math.e)
LANES = 128
VMEM_LIMIT_V7X = 56 * 1024 * 1024
GLA_BLOCK = 256
GLA_HEADS_PER_STEP = 2
GLA_SUB = 16
GLA_MILD_LOG_DECAY = 60.0
ATTN_BLOCK = 512
ATTN_ROWS = 64
PAGES_PER_STEP = 8


def _cparams(*sem):
    return pltpu.CompilerParams(dimension_semantics=sem, vmem_limit_bytes=VMEM_LIMIT_V7X)


def _nt_dot(a, b):
    return lax.dot_general(a, b, (((1,), (1,)), ((), ())), preferred_element_type=F32)


def _tn_dot(a, b):
    return lax.dot_general(a, b, (((0,), (0,)), ((), ())), preferred_element_type=F32)


def _rms(x, g):
    return x * lax.rsqrt(jnp.mean(x * x, axis=-1, keepdims=True) + EPS) * g


def _sigmoid(x):
    return 0.5 * jnp.tanh(0.5 * x) + 0.5


def _norm_kernel(x_ref, g_ref, o_ref):
    o_ref[...] = _rms(x_ref[...], g_ref[...]).astype(o_ref.dtype)


def rmsnorm_cast(x, g, tm=512):
    m, d = x.shape
    tm = min(tm, m)
    assert m % tm == 0
    return pl.pallas_call(
        _norm_kernel,
        out_shape=jax.ShapeDtypeStruct((m, d), BF16),
        grid=(m // tm,),
        in_specs=[pl.BlockSpec((tm, d), lambda i: (i, 0)), pl.BlockSpec((1, d), lambda i: (0, 0))],
        out_specs=pl.BlockSpec((tm, d), lambda i: (i, 0)),
        compiler_params=_cparams("parallel"),
        name="rmsnorm_cast",
    )(x, g.reshape(1, d))


def _proj_kernel(x_ref, w_ref, *rest, shift, n_out, out_scale):
    if shift:
        wn_ref, rest = rest[0], rest[1:]
    o_refs, wb_scr = rest[:n_out], rest[n_out]

    @pl.when(pl.program_id(1) == 0)
    def _():
        w = w_ref[0]
        if shift:
            w = jnp.concatenate([w, wn_ref[0]], axis=1)[:, shift:shift + w.shape[1]]
        wb_scr[...] = w.astype(BF16)

    acc = jnp.dot(x_ref[...], wb_scr[...], preferred_element_type=F32)
    if out_scale != 1.0:
        acc = acc * out_scale
    for o_ref in o_refs:
        o_ref[...] = acc.astype(o_ref.dtype)


def in_proj(x, w3, layer, col0, n, out_dtypes, out_scale=1.0, tm=1024, tn=1024):
    m, k = x.shape
    tm, tn = min(tm, m), min(tn, n)
    base, shift = col0 - col0 % LANES, col0 % LANES
    assert m % tm == 0 and n % tn == 0 and base % tn == 0
    jb, lane_blocks = base // tn, tn // LANES
    in_specs = [pl.BlockSpec((tm, k), lambda j, i: (i, 0)),
                pl.BlockSpec((1, k, tn), lambda j, i: (layer, 0, jb + j))]
    args = [x, w3]
    if shift:
        in_specs.append(pl.BlockSpec((1, k, LANES), lambda j, i: (layer, 0, (jb + j + 1) * lane_blocks)))
        args.append(w3)
    kern = functools.partial(_proj_kernel, shift=shift, n_out=len(out_dtypes), out_scale=out_scale)
    return pl.pallas_call(
        kern,
        out_shape=[jax.ShapeDtypeStruct((m, n), dt) for dt in out_dtypes],
        grid=(n // tn, m // tm),
        in_specs=in_specs,
        out_specs=[pl.BlockSpec((tm, tn), lambda j, i: (i, j)) for _ in out_dtypes],
        scratch_shapes=[pltpu.VMEM((k, tn), BF16)],
        compiler_params=_cparams("parallel", "arbitrary"),
        name="in_proj",
    )(*args)


def _log_sigmoid(z):
    return jnp.minimum(z, 0.0) - jnp.log(1.0 + jnp.exp(-jnp.abs(z)))


def _gla_log_decay(ag, w2, gb):
    z = jnp.dot(ag.astype(BF16), w2.astype(BF16), preferred_element_type=F32)
    return _log_sigmoid(z + gb) * (1.0 / GLA_GATE_TAU)


def _gla_prompt_kernel(q_ref, k_ref, v_ref, r_ref, ag_ref, w2_ref, gb_ref, gn_ref, a_ref, s_ref,
                       b_scr, amat_scr, *, tb, dk, dv, heads):
    @pl.when(pl.program_id(2) == 0)
    def _():
        s_ref[...] = jnp.zeros_like(s_ref)

    for hh in range(heads):
        lk = pl.ds(hh * dk, dk)
        lv = pl.ds(hh * dv, dv)
        _gla_head(q_ref.at[:, lk], k_ref.at[:, lk], v_ref.at[:, lv], r_ref.at[:, lv], ag_ref, w2_ref.at[:, lk],
                  gb_ref.at[:, lk], gn_ref, a_ref.at[:, lv], s_ref.at[0, hh], b_scr.at[hh], amat_scr.at[hh],
                  tb=tb, dk=dk)


def _gla_head(q_ref, k_ref, v_ref, r_ref, ag_ref, w2_ref, gb_ref, gn_ref, a_ref, s_ref, b_scr, amat_scr, *, tb, dk):
    scale = dk ** -0.5
    q = q_ref[...].astype(F32) * scale
    k = k_ref[...].astype(F32)
    v = v_ref[...]
    g = _gla_log_decay(ag_ref[...], w2_ref[...], gb_ref[...])
    row = lax.broadcasted_iota(jnp.int32, (tb, tb), 0)
    col = lax.broadcasted_iota(jnp.int32, (tb, tb), 1)
    causal = col <= row
    tril = jnp.where(causal, 1.0, 0.0).astype(BF16)
    g_hi = g.astype(BF16)
    g_lo = (g - g_hi.astype(F32)).astype(BF16)
    b = (jnp.dot(tril, g_hi, preferred_element_type=F32) + jnp.dot(tril, g_lo, preferred_element_type=F32))
    b_scr[...] = b
    b_last = b[tb - 1:tb, :]
    k_hat = (k * jnp.exp(b_last - b)).astype(BF16)

    s_prev = s_ref[...]
    o = jnp.dot((q * jnp.exp(b)).astype(BF16), s_prev.astype(BF16), preferred_element_type=F32)

    lane = lax.broadcasted_iota(jnp.int32, (GLA_SUB, tb), 1)
    lane_h = lax.broadcasted_iota(jnp.int32, (GLA_SUB, LANES), 1)
    row_h = lax.broadcasted_iota(jnp.int32, (GLA_SUB, LANES), 0)
    mild = jnp.min(b_last) >= -GLA_MILD_LOG_DECAY

    @pl.when(mild)
    def _():
        q_end = (q * jnp.exp(b - b_last)).astype(BF16)
        amat_scr[...] = jnp.where(causal, _nt_dot(q_end, k_hat), 0.0)

    def sub_chunk(i, carry):
        base = pl.multiple_of(i * GLA_SUB, GLA_SUB)
        q_i = q_ref[pl.ds(base, GLA_SUB), :].astype(F32) * scale
        k_i = k_ref[pl.ds(base, GLA_SUB), :].astype(F32)
        b_i = b_scr[pl.ds(base, GLA_SUB), :]
        r_i = b_i[0:1, :]
        q_t = (q_i * jnp.exp(b_i - r_i)).astype(BF16)
        k_t = (k_ref[...].astype(F32) * jnp.exp(jnp.minimum(r_i - b_scr[...], 0.0))).astype(BF16)
        p = jnp.where(lane < base, _nt_dot(q_t, k_t), 0.0)
        lane0 = base % LANES
        d = jnp.zeros((GLA_SUB, LANES), F32)
        for s in range(GLA_SUB):
            e = jnp.exp(jnp.minimum(b_i - b_i[s:s + 1, :], 0.0))
            c = jnp.sum(q_i * e * k_i[s:s + 1, :], axis=-1, keepdims=True)
            d = d + jnp.where(lane_h == lane0 + s, jnp.where(row_h >= s, c, 0.0), 0.0)
        halves = [jnp.where(base // LANES == h, d, 0.0) for h in range(tb // LANES)]
        amat_scr[pl.ds(base, GLA_SUB), :] = p + jnp.concatenate(halves, axis=1)
        return carry

    @pl.when(jnp.logical_not(mild))
    def _():
        lax.fori_loop(0, tb // GLA_SUB, sub_chunk, 0)

    o = o + jnp.dot(amat_scr[...].astype(BF16), v, preferred_element_type=F32)

    decay_col = jnp.broadcast_to(jnp.exp(b_last), (LANES, dk)).T
    dv = v.shape[1]
    decay_full = jnp.concatenate([decay_col] * (dv // LANES), axis=1)
    s_ref[...] = s_prev * decay_full + _tn_dot(k_hat, v)

    rgate = r_ref[...].astype(F32)
    a_ref[...] = (_rms(o, gn_ref[...]) * (rgate * _sigmoid(rgate))).astype(a_ref.dtype)


def gla_prompt(qkvr, ag, w2, gb, gnorm, batch, seq, dk, dv):
    heads = GLA_HEADS
    hg = GLA_HEADS_PER_STEP
    assert heads % hg == 0
    tb = min(GLA_BLOCK, seq)
    nblk = seq // tb
    row = lambda b, h, i: b * nblk + i
    wk, wv = hg * dk, hg * dv
    v0, r0 = (2 * heads * dk) // wv, (2 * heads * dk + heads * dv) // wv
    kern = functools.partial(_gla_prompt_kernel, tb=tb, dk=dk, dv=dv, heads=hg)
    return pl.pallas_call(
        kern,
        out_shape=[jax.ShapeDtypeStruct((batch * seq, heads * dv), BF16),
                   jax.ShapeDtypeStruct((batch, heads, dk, dv), F32)],
        grid=(batch, heads // hg, nblk),
        in_specs=[
            pl.BlockSpec((tb, wk), lambda b, h, i: (row(b, h, i), h)),
            pl.BlockSpec((tb, wk), lambda b, h, i: (row(b, h, i), heads // hg + h)),
            pl.BlockSpec((tb, wv), lambda b, h, i: (row(b, h, i), v0 + h)),
            pl.BlockSpec((tb, wv), lambda b, h, i: (row(b, h, i), r0 + h)),
            pl.BlockSpec((tb, LANES), lambda b, h, i: (row(b, h, i), 0)),
            pl.BlockSpec((LANES, wk), lambda b, h, i: (0, h)),
            pl.BlockSpec((1, wk), lambda b, h, i: (0, h)),
            pl.BlockSpec((1, dv), lambda b, h, i: (0, 0)),
        ],
        out_specs=[pl.BlockSpec((tb, wv), lambda b, h, i: (row(b, h, i), h)),
                   pl.BlockSpec((1, hg, dk, dv), lambda b, h, i: (b, h, 0, 0))],
        scratch_shapes=[pltpu.VMEM((hg, tb, dk), F32), pltpu.VMEM((hg, tb, tb), F32)],
        compiler_params=_cparams("parallel", "parallel", "arbitrary"),
        name="gla_prompt",
    )(qkvr, qkvr, qkvr, qkvr, ag, w2, gb.reshape(1, -1), gnorm.reshape(1, -1))


def _gla_step_kernel(q_ref, k_ref, v_ref, r_ref, ag_ref, w2_ref, gb_ref, gn_ref, s0_ref, a_ref, s_ref, *, nb, dk):
    g = _gla_log_decay(ag_ref[...], w2_ref[...], gb_ref[...])
    reps = LANES // nb
    decay = jnp.exp(g)
    decay_t = jnp.concatenate([decay] * reps, axis=0).T
    k = k_ref[...]
    k_t = jnp.concatenate([k] * reps, axis=0).T
    q = q_ref[...] * (dk ** -0.5)
    q_dec = (q * decay).astype(BF16)
    v = v_ref[...]
    rows = lax.broadcasted_iota(jnp.int32, (nb, v.shape[1]), 0)
    o = jnp.sum(q * k, axis=-1, keepdims=True) * v
    for j in range(nb):
        s0 = s0_ref[j, 0]
        s_ref[j, 0] = s0 * decay_t[:, j:j + 1] + k_t[:, j:j + 1] * v[j:j + 1, :]
        o = o + jnp.where(rows == j, jnp.dot(q_dec, s0.astype(BF16), preferred_element_type=F32), 0.0)
    rgate = r_ref[...]
    a_ref[...] = _rms(o, gn_ref[...]) * (rgate * _sigmoid(rgate))


def gla_step(proj, ag, w2, gb, gnorm, s0, dk, dv):
    heads = GLA_HEADS
    batch = proj.shape[0]
    nb = 8
    kern = functools.partial(_gla_step_kernel, nb=nb, dk=dk)
    return pl.pallas_call(
        kern,
        out_shape=[jax.ShapeDtypeStruct((batch, heads * dv), F32),
                   jax.ShapeDtypeStruct((batch, heads, dk, dv), F32)],
        grid=(batch // nb, heads),
        in_specs=[
            pl.BlockSpec((nb, dk), lambda b, h: (b, h)),
            pl.BlockSpec((nb, dk), lambda b, h: (b, heads + h)),
            pl.BlockSpec((nb, dv), lambda b, h: (b, (2 * heads * dk) // dv + h)),
            pl.BlockSpec((nb, dv), lambda b, h: (b, (2 * heads * dk) // dv + heads + h)),
            pl.BlockSpec((nb, LANES), lambda b, h: (b, 0)),
            pl.BlockSpec((LANES, dk), lambda b, h: (0, h)),
            pl.BlockSpec((1, dk), lambda b, h: (0, h)),
            pl.BlockSpec((1, dv), lambda b, h: (0, 0)),
            pl.BlockSpec((nb, 1, dk, dv), lambda b, h: (b, h, 0, 0)),
        ],
        out_specs=[pl.BlockSpec((nb, dv), lambda b, h: (b, h)),
                   pl.BlockSpec((nb, 1, dk, dv), lambda b, h: (b, h, 0, 0))],
        compiler_params=_cparams("parallel", "parallel"),
        name="gla_step",
    )(proj, proj, proj, proj, ag, w2, gb.reshape(1, -1), gnorm.reshape(1, -1), s0)


def _rel_bucket(dist):
    n = jnp.maximum(dist, 0)
    max_exact = REL_BUCKETS // 2
    large = max_exact + (jnp.log(jnp.maximum(n, 1).astype(F32) / max_exact)
                         / math.log(REL_MAX_DIST / max_exact) * (REL_BUCKETS - max_exact)).astype(jnp.int32)
    large = jnp.minimum(large, REL_BUCKETS - 1)
    return jnp.where(n < max_exact, n, large)


def _lane_tile(x, width):
    return jnp.concatenate([x] * (width // LANES), axis=1)


def _diff_prompt_kernel(lam_ref, q_ref, k_ref, v_ref, w_ref, far_ref, dn_ref, o_ref,
                        m_scr, l_scr, corr_scr, acc_scr, bias_scr, s_scr, p_scr, *, tq, dh, out_scale):
    qi = pl.program_id(2)
    h = pl.program_id(1)
    far = far_ref[h]

    @pl.when(qi == 0)
    def _():
        toep = pltpu.roll(jnp.broadcast_to(w_ref[0], (tq, tq)), 0, 1, stride=1, stride_axis=0)
        row = lax.broadcasted_iota(jnp.int32, (tq, tq), 0)
        col = lax.broadcasted_iota(jnp.int32, (tq, tq), 1)
        bias_scr[0] = jnp.where(col <= row, toep, NEG_BIG)
        bias_scr[1] = jnp.where(col > row, toep, far)

    m_scr[...] = jnp.full_like(m_scr, NEG_BIG)
    l_scr[...] = jnp.zeros_like(l_scr)
    acc_scr[...] = jnp.zeros_like(acc_scr)
    q = q_ref[...]

    def visit(kb, bias_rows):
        start = pl.multiple_of(kb * tq, tq)
        kk = k_ref[pl.ds(start, tq), :]
        vv = v_ref[pl.ds(start, tq), :]
        for c in range(2):
            s_scr[c] = _nt_dot(q[:, c * dh:(c + 1) * dh], kk[:, c * dh:(c + 1) * dh])

        def chunk(r, carry):
            rows = pl.ds(r * ATTN_ROWS, ATTN_ROWS)
            bias = bias_rows(rows)
            for c in range(2):
                s = s_scr[c, rows, :] + bias
                m_old = m_scr[c, rows, :]
                m_new = jnp.maximum(m_old, jnp.max(s, axis=-1, keepdims=True))
                corr = jnp.exp2(m_old - m_new)
                p = jnp.exp2(s - _lane_tile(m_new, tq))
                l_scr[c, rows, :] = l_scr[c, rows, :] * corr + jnp.sum(p, axis=-1, keepdims=True)
                corr_scr[c, rows, :] = corr
                m_scr[c, rows, :] = m_new
                p_scr[c, rows, :] = p.astype(BF16)
            return carry

        for r in range(tq // ATTN_ROWS):
            chunk(r, 0)
        for c in range(2):
            acc_scr[c] = (acc_scr[c] * _lane_tile(corr_scr[c], 2 * dh)
                          + jnp.dot(p_scr[c], vv, preferred_element_type=F32))

    def far_block(kb, carry):
        visit(kb, lambda rows: far)
        return carry

    lax.fori_loop(0, jnp.maximum(qi - 1, 0), far_block, 0)

    @pl.when(qi >= 1)
    def _():
        visit(qi - 1, lambda rows: bias_scr[1, rows, :])

    visit(qi, lambda rows: bias_scr[0, rows, :])
    lam = lam_ref[0]
    o = (acc_scr[0] / _lane_tile(l_scr[0], 2 * dh) - lam * (acc_scr[1] / _lane_tile(l_scr[1], 2 * dh)))
    o_ref[...] = (_rms(o, dn_ref[...]) * out_scale).astype(o_ref.dtype)


def diff_attention_prompt(q, k, v, lam, rel_table, diff_norm, batch, seq, lam_init):
    heads = DIFF_HEADS
    dh = q.shape[1] // (2 * heads)
    tq = min(ATTN_BLOCK, seq)
    assert tq >= REL_MAX_DIST or seq == tq
    nq = seq // tq
    dist = (tq - jnp.arange(tq, dtype=jnp.int32)) % tq
    w = (rel_table[_rel_bucket(dist)].astype(F32) * LOG2E).T.reshape(heads, 1, tq)
    far = rel_table[REL_BUCKETS - 1].astype(F32) * LOG2E
    kern = functools.partial(_diff_prompt_kernel, tq=tq, dh=dh, out_scale=1.0 - lam_init)
    return pl.pallas_call(
        kern,
        out_shape=jax.ShapeDtypeStruct((batch * seq, heads * 2 * dh), BF16),
        grid=(batch, heads, nq),
        in_specs=[
            pl.BlockSpec(memory_space=pltpu.SMEM),
            pl.BlockSpec((tq, 2 * dh), lambda b, h, i: (b * nq + i, h)),
            pl.BlockSpec((seq, 2 * dh), lambda b, h, i: (b, h)),
            pl.BlockSpec((seq, 2 * dh), lambda b, h, i: (b, h)),
            pl.BlockSpec((1, 1, tq), lambda b, h, i: (h, 0, 0)),
            pl.BlockSpec(memory_space=pltpu.SMEM),
            pl.BlockSpec((1, 2 * dh), lambda b, h, i: (0, 0)),
        ],
        out_specs=pl.BlockSpec((tq, 2 * dh), lambda b, h, i: (b * nq + i, h)),
        scratch_shapes=[pltpu.VMEM((2, tq, LANES), F32), pltpu.VMEM((2, tq, LANES), F32),
                        pltpu.VMEM((2, tq, LANES), F32), pltpu.VMEM((2, tq, 2 * dh), F32),
                        pltpu.VMEM((2, tq, tq), F32), pltpu.VMEM((2, tq, tq), F32), pltpu.VMEM((2, tq, tq), BF16)],
        compiler_params=_cparams("arbitrary", "arbitrary", "arbitrary"),
        name="diff_attention_prompt",
    )(lam.reshape(1), q, k, v, w, far, diff_norm.reshape(1, -1))


def _diff_paged_kernel(pt_ref, lam_ref, q_ref, *rest, n_pages, pages, out_scale, heads):
    k_refs = rest[:pages]
    v_refs = rest[pages:2 * pages]
    bias_ref, kn_ref, vn_ref, bn_ref, dn_ref, o_ref, m_scr, l_scr, acc_scr = rest[2 * pages:]
    del pt_ref
    jg = pl.program_id(1)

    @pl.when(jg == 0)
    def _():
        m_scr[...] = jnp.full_like(m_scr, NEG_BIG)
        l_scr[...] = jnp.zeros_like(l_scr)
        acc_scr[...] = jnp.zeros_like(acc_scr)

    q = q_ref[0]

    def update(ks, vs, biases):
        ss = [_nt_dot(q, kk.astype(BF16)) + bias for kk, bias in zip(ks, biases)]
        m_old = m_scr[...]
        m_new = m_old
        for s in ss:
            m_new = jnp.maximum(m_new, jnp.max(s, axis=-1, keepdims=True))
        corr = jnp.exp(m_old - m_new)
        l_new = l_scr[...] * corr
        acc = acc_scr[...] * corr
        for s, vv in zip(ss, vs):
            p = jnp.exp(s - m_new)
            l_new = l_new + jnp.sum(p, axis=-1, keepdims=True)
            acc = acc + jnp.dot(p.astype(BF16), vv.astype(BF16), preferred_element_type=F32)
        l_scr[...] = l_new
        acc_scr[...] = acc
        m_scr[...] = m_new

    def as_rows(ref):
        x = ref[0, 0]
        return x.reshape(x.shape[0] * x.shape[1], x.shape[2])

    biases = [bias_ref[(jg * pages + i == n_pages - 1).astype(jnp.int32)] for i in range(pages)]
    update([as_rows(r) for r in k_refs], [as_rows(r) for r in v_refs], biases)

    @pl.when(jg == pl.num_programs(1) - 1)
    def _():
        update([kn_ref[0]], [vn_ref[0]], [bn_ref[...]])
        o = acc_scr[...] / l_scr[...]
        out = o[:heads] - lam_ref[0] * o[heads:]
        o_ref[0] = _rms(out, dn_ref[...]) * out_scale


def diff_attention_paged(q, k_new, v_new, lam, cache_k, cache_v, layer, page_table, rel_table, diff_norm, lam_init):
    heads = DIFF_HEADS
    batch = q.shape[0]
    dh = q.shape[1] // (2 * heads)
    n_pages = page_table.shape[1]
    page = cache_k.shape[2]
    past = n_pages * page
    pages = math.gcd(PAGES_PER_STEP, n_pages)
    assert page >= REL_MAX_DIST
    rows = page * heads
    q4 = (q.reshape(batch, heads, 2, dh) * (dh ** -0.5)).astype(BF16)
    zeros = jnp.zeros_like(q4[:, :, 0])
    q_mat = jnp.concatenate([jnp.concatenate([q4[:, :, 0], zeros], axis=-1),
                             jnp.concatenate([zeros, q4[:, :, 1]], axis=-1)], axis=1)
    tok = jnp.arange(page, dtype=jnp.int32)
    dist_last = past - ((n_pages - 1) * page + tok)
    b_far = jnp.broadcast_to(rel_table[REL_BUCKETS - 1].astype(F32)[:, None], (heads, page))
    b_last = rel_table[_rel_bucket(dist_last)].astype(F32).T
    own = (jnp.arange(heads)[:, None, None] == jnp.arange(heads)[None, None, :])
    def expand(bh):
        t = jnp.where(own, bh[:, :, None], NEG_BIG).reshape(heads, rows)
        return jnp.concatenate([t, t], axis=0)
    bias = jnp.stack([expand(b_far), expand(b_last)])
    pad = LANES - heads
    kn = jnp.pad(k_new.reshape(batch, heads, 2 * dh), ((0, 0), (0, pad), (0, 0)))
    vn = jnp.pad(v_new.reshape(batch, heads, 2 * dh), ((0, 0), (0, pad), (0, 0)))
    b0 = jnp.where(jnp.arange(heads)[:, None] == jnp.arange(LANES)[None, :],
                   rel_table[0].astype(F32)[:, None], NEG_BIG)
    bn = jnp.concatenate([b0, b0], axis=0)

    def page_spec(i):
        return pl.BlockSpec((1, 1) + cache_k.shape[2:],
                            lambda b, j, pt: (layer, pt[b * n_pages + j * pages + i], 0, 0, 0))

    kern = functools.partial(_diff_paged_kernel, n_pages=n_pages, pages=pages, out_scale=1.0 - lam_init, heads=heads)
    out = pl.pallas_call(
        kern,
        out_shape=jax.ShapeDtypeStruct((batch, heads, 2 * dh), F32),
        grid_spec=pltpu.PrefetchScalarGridSpec(
            num_scalar_prefetch=1,
            grid=(batch, n_pages // pages),
            in_specs=[pl.BlockSpec(memory_space=pltpu.SMEM),
                      pl.BlockSpec((1, 2 * heads, 2 * dh), lambda b, j, pt: (b, 0, 0))]
                     + [page_spec(i) for i in range(pages)] + [page_spec(i) for i in range(pages)]
                     + [pl.BlockSpec((2, 2 * heads, rows), lambda b, j, pt: (0, 0, 0)),
                        pl.BlockSpec((1, LANES, 2 * dh), lambda b, j, pt: (b, 0, 0)),
                        pl.BlockSpec((1, LANES, 2 * dh), lambda b, j, pt: (b, 0, 0)),
                        pl.BlockSpec((2 * heads, LANES), lambda b, j, pt: (0, 0)),
                        pl.BlockSpec((1, 2 * dh), lambda b, j, pt: (0, 0))],
            out_specs=pl.BlockSpec((1, heads, 2 * dh), lambda b, j, pt: (b, 0, 0)),
            scratch_shapes=[pltpu.VMEM((2 * heads, 1), F32), pltpu.VMEM((2 * heads, 1), F32),
                            pltpu.VMEM((2 * heads, 2 * dh), F32)],
        ),
        compiler_params=_cparams("parallel", "arbitrary"),
        name="diff_attention_paged",
    )(page_table.reshape(-1), lam.reshape(1), q_mat, *([cache_k] * pages), *([cache_v] * pages),
      bias, kn, vn, bn, diff_norm.reshape(1, -1))
    return out.reshape(batch, heads * 2 * dh)


def _mem_attn_kernel(q_ref, k_ref, v_ref, o_ref, *, scale):
    q = q_ref[0]
    tq = q.shape[0]
    if tq < 8:
        q = jnp.broadcast_to(q[0:1], (8, q.shape[1]))
    s = _nt_dot(q.astype(BF16), k_ref[0].astype(BF16)) * scale
    p = jnp.exp(s - jnp.max(s, axis=-1, keepdims=True))
    o = jnp.dot(p.astype(BF16), v_ref[0].astype(BF16), preferred_element_type=F32)
    o = o / jnp.sum(p, axis=-1, keepdims=True)
    o_ref[0] = o[:tq].astype(o_ref.dtype)


def memory_attention(q3, col0, mem_k, mem_v, out_dtype, tq=2048):
    batch, seq, _ = q3.shape
    mtok, width = mem_k.shape[1:]
    md = width // MEM_HEADS
    tq = min(tq, seq)
    kern = functools.partial(_mem_attn_kernel, scale=md ** -0.5)
    return pl.pallas_call(
        kern,
        out_shape=jax.ShapeDtypeStruct((batch, seq, width), out_dtype),
        grid=(batch, MEM_HEADS, seq // tq),
        in_specs=[pl.BlockSpec((1, tq, md), lambda b, h, i: (b, i, col0 + h)),
                  pl.BlockSpec((1, mtok, md), lambda b, h, i: (b, 0, h)),
                  pl.BlockSpec((1, mtok, md), lambda b, h, i: (b, 0, h))],
        out_specs=pl.BlockSpec((1, tq, md), lambda b, h, i: (b, i, h)),
        compiler_params=_cparams("parallel", "parallel", "parallel"),
        name="memory_attention",
    )(q3, mem_k, mem_v)


def _mem_step_kernel(q_ref, k_ref, v_ref, o_ref, *, scale, heads):
    kk = k_ref[0, 0]
    rows = kk.shape[0] * kk.shape[1]
    k2 = kk.reshape(rows, kk.shape[2]).astype(BF16)
    v2 = v_ref[0, 0].reshape(rows, kk.shape[2]).astype(BF16)
    s = _nt_dot(q_ref[0].astype(BF16), k2) * scale
    lane = lax.broadcasted_iota(jnp.int32, s.shape, 1)
    qrow = lax.broadcasted_iota(jnp.int32, s.shape, 0)
    s = jnp.where(lane % heads == qrow, s, NEG_BIG)
    p = jnp.exp(s - jnp.max(s, axis=-1, keepdims=True))
    o = jnp.dot(p.astype(BF16), v2, preferred_element_type=F32)
    o_ref[0] = o / jnp.sum(p, axis=-1, keepdims=True)


def memory_attention_step(q, cache_k, cache_v, layer):
    batch = q.shape[0]
    _, _, mtok, heads, md = cache_k.shape
    qrows = 8
    assert heads <= qrows
    q8 = jnp.pad(q.reshape(batch, heads, md), ((0, 0), (0, qrows - heads), (0, 0)))
    kern = functools.partial(_mem_step_kernel, scale=md ** -0.5, heads=heads)
    cache_spec = pl.BlockSpec((1, 1, mtok, heads, md), lambda b: (layer, b, 0, 0, 0))
    out = pl.pallas_call(
        kern,
        out_shape=jax.ShapeDtypeStruct((batch, qrows, md), F32),
        grid=(batch,),
        in_specs=[pl.BlockSpec((1, qrows, md), lambda b: (b, 0, 0)), cache_spec, cache_spec],
        out_specs=pl.BlockSpec((1, qrows, md), lambda b: (b, 0, 0)),
        compiler_params=_cparams("parallel"),
        name="memory_attention_step",
    )(q8, cache_k, cache_v)
    return out[:, :heads].reshape(batch, heads * md)


def _merge_kernel(a_ref, b_ref, c_ref, g0_ref, g1_ref, g2_ref, x_ref, w_ref, npost_ref, npre_ref, x1_ref, h2_ref):
    merged = (_sigmoid(g0_ref[...].astype(F32)) * a_ref[...].astype(F32)
              + _sigmoid(g1_ref[...].astype(F32)) * b_ref[...].astype(F32)
              + _sigmoid(g2_ref[...].astype(F32)) * c_ref[...].astype(F32))
    y = jnp.dot(merged.astype(BF16), w_ref[...], preferred_element_type=F32)
    x1 = x_ref[...] + _rms(y, npost_ref[...])
    x1_ref[...] = x1
    h2_ref[...] = _rms(x1, npre_ref[...]).astype(h2_ref.dtype)


def merge_out_proj(a, b, c, gates_arr, gate_col0, x, w_out, norm_post, norm_pre_ffn, tm=256):
    m, d = x.shape
    tm = min(tm, m)
    rowblk = lambda i: (i, 0)
    vec = pl.BlockSpec((1, d), lambda i: (0, 0))
    return pl.pallas_call(
        _merge_kernel,
        out_shape=[jax.ShapeDtypeStruct((m, d), F32), jax.ShapeDtypeStruct((m, d), BF16)],
        grid=(m // tm,),
        in_specs=[pl.BlockSpec((tm, d), rowblk), pl.BlockSpec((tm, d), rowblk), pl.BlockSpec((tm, d), rowblk),
                  pl.BlockSpec((tm, d), lambda i: (i, gate_col0)),
                  pl.BlockSpec((tm, d), lambda i: (i, gate_col0 + 1)),
                  pl.BlockSpec((tm, d), lambda i: (i, gate_col0 + 2)),
                  pl.BlockSpec((tm, d), rowblk),
                  pl.BlockSpec((d, d), lambda i: (0, 0)),
                  vec, vec],
        out_specs=[pl.BlockSpec((tm, d), rowblk), pl.BlockSpec((tm, d), rowblk)],
        compiler_params=_cparams("parallel"),
        name="merge_out_proj",
    )(a, b, c, gates_arr, gates_arr, gates_arr, x, w_out, norm_post.reshape(1, d), norm_pre_ffn.reshape(1, d))


def _gelu_tanh(x):
    return 0.5 * x * (1.0 + jnp.tanh(math.sqrt(2.0 / math.pi) * (x + 0.044715 * (x * x * x))))


def _ffn_up_seq_kernel(h_ref, wg_ref, wv_ref, cwg_ref, cwv_ref, cbg_ref, cbv_ref, stg_ref, stv_ref,
                       act_ref, tailg_ref, tailv_ref, carryg, carryv, *, tiles_per_seq):
    mi = pl.program_id(1)
    h = h_ref[...]
    tm = h.shape[0]
    row = lax.broadcasted_iota(jnp.int32, (tm, 1), 0)
    seq_start = mi % tiles_per_seq == 0

    def conv(w_ref, cw_ref, cb_ref, st_ref, carry, tail_ref):
        u = jnp.dot(h, w_ref[...], preferred_element_type=F32)
        st = st_ref[0]
        prev2 = jnp.where(seq_start, st[0:1, :], carry[6:7, :])
        prev1 = jnp.where(seq_start, st[1:2, :], carry[7:8, :])
        u1 = jnp.where(row >= 1, pltpu.roll(u, 1, 0), prev1)
        u2 = jnp.where(row >= 2, pltpu.roll(u, 2, 0), jnp.where(row == 1, prev1, prev2))
        cw = cw_ref[...]
        tail = u[tm - 8:tm, :]
        carry[...] = tail
        tail_ref[0] = tail
        return cb_ref[...] + u2 * cw[0:1, :] + u1 * cw[1:2, :] + u * cw[2:3, :]

    gate = conv(wg_ref, cwg_ref, cbg_ref, stg_ref, carryg, tailg_ref)
    val = conv(wv_ref, cwv_ref, cbv_ref, stv_ref, carryv, tailv_ref)
    act_ref[...] = (_gelu_tanh(gate) * val).astype(act_ref.dtype)


def ffn_up_seq(h2, w_up, conv_w, conv_b, conv_state, seq, tm=1024, tn=512):
    m, d = h2.shape
    f = w_up.shape[1] // 2
    tm = min(tm, seq)
    tiles_per_seq = seq // tm
    nj = f // tn
    kern = functools.partial(_ffn_up_seq_kernel, tiles_per_seq=tiles_per_seq)
    cw = conv_w.astype(F32)
    cb = conv_b.reshape(1, -1).astype(F32)
    return pl.pallas_call(
        kern,
        out_shape=[jax.ShapeDtypeStruct((m, f), BF16),
                   jax.ShapeDtypeStruct((m // tm, 8, f), F32),
                   jax.ShapeDtypeStruct((m // tm, 8, f), F32)],
        grid=(nj, m // tm),
        in_specs=[pl.BlockSpec((tm, d), lambda j, i: (i, 0)),
                  pl.BlockSpec((d, tn), lambda j, i: (0, j)),
                  pl.BlockSpec((d, tn), lambda j, i: (0, nj + j)),
                  pl.BlockSpec((cw.shape[0], tn), lambda j, i: (0, j)),
                  pl.BlockSpec((cw.shape[0], tn), lambda j, i: (0, nj + j)),
                  pl.BlockSpec((1, tn), lambda j, i: (0, j)),
                  pl.BlockSpec((1, tn), lambda j, i: (0, nj + j)),
                  pl.BlockSpec((1, 2, tn), lambda j, i: (i // tiles_per_seq, 0, j)),
                  pl.BlockSpec((1, 2, tn), lambda j, i: (i // tiles_per_seq, 0, nj + j))],
        out_specs=[pl.BlockSpec((tm, tn), lambda j, i: (i, j)),
                   pl.BlockSpec((1, 8, tn), lambda j, i: (i, 0, j)),
                   pl.BlockSpec((1, 8, tn), lambda j, i: (i, 0, j))],
        scratch_shapes=[pltpu.VMEM((8, tn), F32), pltpu.VMEM((8, tn), F32)],
        compiler_params=_cparams("parallel", "arbitrary"),
        name="ffn_up_seq",
    )(h2, w_up, w_up, cw, cw, cb, cb, conv_state, conv_state)


def _ffn_up_step_kernel(h_ref, wg_ref, wv_ref, cwg_ref, cwv_ref, cbg_ref, cbv_ref, stg_ref, stv_ref,
                        act_ref, ug_ref, uv_ref):
    h = h_ref[...]

    def conv(w_ref, cw_ref, cb_ref, st_ref, u_ref):
        u = jnp.dot(h, w_ref[...], preferred_element_type=F32)
        u_ref[...] = u
        cw = cw_ref[...]
        return cb_ref[...] + st_ref[0] * cw[0:1, :] + st_ref[1] * cw[1:2, :] + u * cw[2:3, :]

    gate = conv(wg_ref, cwg_ref, cbg_ref, stg_ref, ug_ref)
    val = conv(wv_ref, cwv_ref, cbv_ref, stv_ref, uv_ref)
    act_ref[...] = (_gelu_tanh(gate) * val).astype(act_ref.dtype)


def ffn_up_step(h2, w_up, conv_w, conv_b, conv_state_t, tn=512):
    m, d = h2.shape
    f = w_up.shape[1] // 2
    nj = f // tn
    cw = conv_w.astype(F32)
    cb = conv_b.reshape(1, -1).astype(F32)
    return pl.pallas_call(
        _ffn_up_step_kernel,
        out_shape=[jax.ShapeDtypeStruct((m, f), BF16), jax.ShapeDtypeStruct((m, f), F32),
                   jax.ShapeDtypeStruct((m, f), F32)],
        grid=(nj,),
        in_specs=[pl.BlockSpec((m, d), lambda j: (0, 0)),
                  pl.BlockSpec((d, tn), lambda j: (0, j)),
                  pl.BlockSpec((d, tn), lambda j: (0, nj + j)),
                  pl.BlockSpec((cw.shape[0], tn), lambda j: (0, j)),
                  pl.BlockSpec((cw.shape[0], tn), lambda j: (0, nj + j)),
                  pl.BlockSpec((1, tn), lambda j: (0, j)),
                  pl.BlockSpec((1, tn), lambda j: (0, nj + j)),
                  pl.BlockSpec((2, m, tn), lambda j: (0, 0, j)),
                  pl.BlockSpec((2, m, tn), lambda j: (0, 0, nj + j))],
        out_specs=[pl.BlockSpec((m, tn), lambda j: (0, j)),
                   pl.BlockSpec((m, tn), lambda j: (0, j)),
                   pl.BlockSpec((m, tn), lambda j: (0, j))],
        compiler_params=_cparams("parallel"),
        name="ffn_up_step",
    )(h2, w_up, w_up, cw, cw, cb, cb, conv_state_t, conv_state_t)


def _ffn_down_kernel(act_ref, w_ref, x_ref, n_ref, y_ref):
    f = jnp.dot(act_ref[...], w_ref[...], preferred_element_type=F32)
    y_ref[...] = x_ref[...] + _rms(f, n_ref[...])


def ffn_down(act, w_down, x1, norm_post, tm=512):
    m, f = act.shape
    d = w_down.shape[1]
    tm = min(tm, m)
    assert m % tm == 0
    return pl.pallas_call(
        _ffn_down_kernel,
        out_shape=jax.ShapeDtypeStruct((m, d), F32),
        grid=(m // tm,),
        in_specs=[pl.BlockSpec((tm, f), lambda i: (i, 0)),
                  pl.BlockSpec((f, d), lambda i: (0, 0), pipeline_mode=pl.Buffered(1)),
                  pl.BlockSpec((tm, d), lambda i: (i, 0)),
                  pl.BlockSpec((1, d), lambda i: (0, 0))],
        out_specs=pl.BlockSpec((tm, d), lambda i: (i, 0)),
        compiler_params=_cparams("parallel"),
        name="ffn_down",
    )(act, w_down, x1, norm_post.reshape(1, d))


def kernel(x_prompt, x_sample, cache_k, cache_v, state_gla, cache_mem_k, cache_mem_v, state_ffn_conv, page_table, mem_prompt, rel_bias_table, norm_pre_mix, norm_post_mix, norm_pre_ffn, norm_post_ffn, norm_mem, w_in, w_gla_gate2, b_gla_gate, gla_norm, diff_lambda_q1, diff_lambda_k1, diff_lambda_q2, diff_lambda_k2, diff_norm, w_mem_kv, w_out, w_up, ffn_conv_w, ffn_conv_b, w_down):
    depth = w_in.shape[0]
    bp, tp, d = x_prompt.shape
    bs, ts, _ = x_sample.shape
    assert ts == 1
    dk, dv = d // (2 * GLA_HEADS), d // GLA_HEADS
    mtok = mem_prompt.shape[1]
    f2 = w_up.shape[2]

    yp = x_prompt.reshape(bp * tp, d)
    ys = x_sample.reshape(bs * ts, d)
    outs = {n: [] for n in ("kp", "vp", "gp", "mk", "mv", "cp", "ks", "vs", "gs", "cs")}
    for l in range(depth):
        lam_init = 0.8 - 0.6 * math.exp(-0.3 * l)
        lam = (jnp.exp(jnp.sum(diff_lambda_q1[l].astype(F32) * diff_lambda_k1[l].astype(F32)))
               - jnp.exp(jnp.sum(diff_lambda_q2[l].astype(F32) * diff_lambda_k2[l].astype(F32))) + lam_init)
        rank = w_gla_gate2.shape[1]
        w2 = jnp.pad(w_gla_gate2[l].astype(F32), ((0, LANES - rank), (0, 0)))
        c_ag = 3 * d
        c_qd = c_ag + rank
        c_kd, c_vd, c_qm = c_qd + d, c_qd + 2 * d, c_qd + 3 * d
        q_scale = (d // (2 * DIFF_HEADS)) ** -0.5 * LOG2E
        w_out_b, w_up_b, w_down_b = w_out[l].astype(BF16), w_up[l].astype(BF16), w_down[l].astype(BF16)

        hm = rmsnorm_cast(mem_prompt.reshape(bp * mtok, d), norm_mem[l])
        (mem_k,) = in_proj(hm, w_mem_kv, l, 0, d, [F32])
        (mem_v,) = in_proj(hm, w_mem_kv, l, d, d, [F32])
        h = rmsnorm_cast(yp, norm_pre_mix[l])
        (qkvr,) = in_proj(h, w_in, l, 0, 3 * d, [BF16])
        (ag,) = in_proj(h, w_in, l, c_ag, LANES, [F32])
        (qd,) = in_proj(h, w_in, l, c_qd, d, [BF16], out_scale=q_scale)
        k_rows, kd_b = in_proj(h, w_in, l, c_kd, d, [F32, BF16])
        v_rows, vd_b = in_proj(h, w_in, l, c_vd, d, [F32, BF16])
        (qm_gates,) = in_proj(h, w_in, l, c_qm, 4 * d, [BF16])
        a, gla_state = gla_prompt(qkvr, ag, w2, b_gla_gate[l], gla_norm[l], bp, tp, dk, dv)
        b = diff_attention_prompt(qd, kd_b, vd_b, lam, rel_bias_table, diff_norm[l], bp, tp, lam_init)
        c = memory_attention(qm_gates.reshape(bp, tp, -1), 0, mem_k.reshape(bp, mtok, d),
                             mem_v.reshape(bp, mtok, d), BF16)
        x1, h2 = merge_out_proj(a, b, c.reshape(bp * tp, d), qm_gates, 1, yp, w_out_b, norm_post_mix[l],
                                norm_pre_ffn[l])
        conv0 = jnp.zeros((bp, ffn_conv_w.shape[1] - 1, f2), F32)
        act, tail_g, tail_v = ffn_up_seq(h2, w_up_b, ffn_conv_w[l], ffn_conv_b[l], conv0, tp)
        yp = ffn_down(act, w_down_b, x1, norm_post_ffn[l])
        tiles = tail_g.shape[0] // bp
        tail = jnp.concatenate([tail_g, tail_v], axis=-1).reshape(bp, tiles, 8, f2)
        outs["kp"].append(k_rows.reshape(bp, tp, DIFF_HEADS, -1))
        outs["vp"].append(v_rows.reshape(bp, tp, DIFF_HEADS, -1))
        outs["gp"].append(gla_state)
        outs["mk"].append(mem_k.reshape(bp, mtok, MEM_HEADS, -1))
        outs["mv"].append(mem_v.reshape(bp, mtok, MEM_HEADS, -1))
        outs["cp"].append(tail[:, -1, 6:8])

        hs = rmsnorm_cast(ys, norm_pre_mix[l])
        (qkvr_s,) = in_proj(hs, w_in, l, 0, 3 * d, [F32])
        (ag_s,) = in_proj(hs, w_in, l, c_ag, LANES, [F32])
        (rest_s,) = in_proj(hs, w_in, l, c_qd, 7 * d, [F32])
        kd_s, vd_s = rest_s[:, d:2 * d], rest_s[:, 2 * d:3 * d]
        a_s, gla_state_s = gla_step(qkvr_s, ag_s, w2, b_gla_gate[l], gla_norm[l], state_gla[l], dk, dv)
        b_s = diff_attention_paged(rest_s[:, :d], kd_s, vd_s, lam, cache_k, cache_v, l, page_table,
                                   rel_bias_table, diff_norm[l], lam_init)
        c_s = memory_attention_step(rest_s[:, 3 * d:4 * d], cache_mem_k, cache_mem_v, l)
        x1_s, h2_s = merge_out_proj(a_s, b_s, c_s, rest_s, 4, ys, w_out_b,
                                    norm_post_mix[l], norm_pre_ffn[l])
        st = state_ffn_conv[l].astype(F32)
        act_s, ug_s, uv_s = ffn_up_step(h2_s, w_up_b, ffn_conv_w[l], ffn_conv_b[l], jnp.swapaxes(st, 0, 1))
        ys = ffn_down(act_s, w_down_b, x1_s, norm_post_ffn[l])
        u_s = jnp.concatenate([ug_s, uv_s], axis=-1)
        outs["ks"].append(kd_s.reshape(bs, ts, DIFF_HEADS, -1))
        outs["vs"].append(vd_s.reshape(bs, ts, DIFF_HEADS, -1))
        outs["gs"].append(gla_state_s)
        outs["cs"].append(jnp.stack([st[:, 1], u_s], axis=1))

    st = lambda n: jnp.stack(outs[n])
    return (yp.reshape(bp, tp, d), ys.reshape(bs, ts, d), st("kp"), st("vp"), st("gp"), st("mk"), st("mv"),
            st("cp"), st("ks"), st("vs"), st("gs"), st("cs"))
```

```python
import functools
import math

import jax
import jax.numpy as jnp
from jax import lax
from jax.experimental import pallas as pl
from jax.experimental.pallas import tpu as pltpu

F32 = jnp.float32
BF16 = jnp.bfloat16

GLA_HEADS = 4
GLA_GATE_TAU = 16.0
DIFF_HEADS = 8
MEM_HEADS = 4
REL_BUCKETS = 32
REL_MAX_DIST = 128
N_BRANCH = 3
EPS = 1e-6

NEG_BIG = -1e30
LOG2E = math.log2(math.e)
LANES = 128
VMEM_LIMIT_V7X = 56 * 1024 * 1024
GLA_BLOCK = 256
GLA_HEADS_PER_STEP = 2
GLA_SUB = 16
GLA_MILD_LOG_DECAY = 60.0
ATTN_BLOCK = 512
ATTN_ROWS = 64
PAGES_PER_STEP = 8
MERGE_SUB_ROWS = 128


def _cparams(*sem):
    return pltpu.CompilerParams(dimension_semantics=sem, vmem_limit_bytes=VMEM_LIMIT_V7X)


def _nt_dot(a, b):
    return lax.dot_general(a, b, (((1,), (1,)), ((), ())), preferred_element_type=F32)


def _tn_dot(a, b):
    return lax.dot_general(a, b, (((0,), (0,)), ((), ())), preferred_element_type=F32)


def _rms(x, g):
    return x * lax.rsqrt(jnp.mean(x * x, axis=-1, keepdims=True) + EPS) * g


def _sigmoid(x):
    return 0.5 * jnp.tanh(0.5 * x) + 0.5


def _norm_kernel(x_ref, g_ref, o_ref):
    o_ref[...] = _rms(x_ref[...], g_ref[...]).astype(o_ref.dtype)


def rmsnorm_cast(x, g, tm=512):
    m, d = x.shape
    tm = min(tm, m)
    assert m % tm == 0
    return pl.pallas_call(
        _norm_kernel,
        out_shape=jax.ShapeDtypeStruct((m, d), BF16),
        grid=(m // tm,),
        in_specs=[pl.BlockSpec((tm, d), lambda i: (i, 0)), pl.BlockSpec((1, d), lambda i: (0, 0))],
        out_specs=pl.BlockSpec((tm, d), lambda i: (i, 0)),
        compiler_params=_cparams("parallel"),
        name="rmsnorm_cast",
    )(x, g.reshape(1, d))


def _proj_kernel(x_ref, w_ref, *rest, shift, n_out, out_scale, by_rows):
    if shift:
        wn_ref, rest = rest[0], rest[1:]
    o_refs, wb_scr = rest[:n_out], rest[n_out]

    @pl.when(pl.program_id(1) == 0)
    def _():
        if shift:
            wb_scr[...] = jnp.concatenate([w_ref[0, shift:, :], wn_ref[0]], axis=0).astype(BF16)
        else:
            wb_scr[...] = w_ref[0].astype(BF16)

    acc = _nt_dot(x_ref[...], wb_scr[...]) if by_rows else jnp.dot(x_ref[...], wb_scr[...],
                                                                  preferred_element_type=F32)
    if out_scale != 1.0:
        acc = acc * out_scale
    for o_ref in o_refs:
        o_ref[...] = acc.astype(o_ref.dtype)


def in_proj(x, w3, layer, col0, n, out_dtypes, out_scale=1.0, by_rows=False, tm=1024, tn=1024):
    m, k = x.shape
    tm, tn = min(tm, m), min(tn, n)
    base, shift = col0 - col0 % tn, col0 % tn
    assert m % tm == 0 and n % tn == 0 and shift % 8 == 0 and tn % max(shift, 1) == 0
    assert by_rows or shift == 0
    jb = base // tn
    w_block = (1, tn, k) if by_rows else (1, k, tn)
    w_index = (lambda j, i: (layer, jb + j, 0)) if by_rows else (lambda j, i: (layer, 0, jb + j))
    in_specs = [pl.BlockSpec((tm, k), lambda j, i: (i, 0)), pl.BlockSpec(w_block, w_index)]
    args = [x, w3]
    if shift:
        in_specs.append(pl.BlockSpec((1, shift, k), lambda j, i: (layer, (jb + j + 1) * (tn // shift), 0)))
        args.append(w3)
    kern = functools.partial(_proj_kernel, shift=shift, n_out=len(out_dtypes), out_scale=out_scale, by_rows=by_rows)
    return pl.pallas_call(
        kern,
        out_shape=[jax.ShapeDtypeStruct((m, n), dt) for dt in out_dtypes],
        grid=(n // tn, m // tm),
        in_specs=in_specs,
        out_specs=[pl.BlockSpec((tm, tn), lambda j, i: (i, j)) for _ in out_dtypes],
        scratch_shapes=[pltpu.VMEM((tn, k) if by_rows else (k, tn), BF16)],
        compiler_params=_cparams("parallel", "arbitrary"),
        name="in_proj",
    )(*args)


def _log_sigmoid(z):
    return jnp.minimum(z, 0.0) - jnp.log(1.0 + jnp.exp(-jnp.abs(z)))


def _gla_log_decay(ag, w2, gb):
    z = jnp.dot(ag.astype(BF16), w2.astype(BF16), preferred_element_type=F32)
    return _log_sigmoid(z + gb) * (1.0 / GLA_GATE_TAU)


def _gla_prompt_kernel(q_ref, k_ref, v_ref, r_ref, ag_ref, w2_ref, gb_ref, gn_ref, a_ref, s_ref,
                       b_scr, amat_scr, *, tb, dk, dv, heads):
    @pl.when(pl.program_id(2) == 0)
    def _():
        s_ref[...] = jnp.zeros_like(s_ref)

    for hh in range(heads):
        lk = pl.ds(hh * dk, dk)
        lv = pl.ds(hh * dv, dv)
        _gla_head(q_ref.at[:, lk], k_ref.at[:, lk], v_ref.at[:, lv], r_ref.at[:, lv], ag_ref, w2_ref.at[:, lk],
                  gb_ref.at[:, lk], gn_ref, a_ref.at[:, lv], s_ref.at[0, hh], b_scr.at[hh], amat_scr.at[hh],
                  tb=tb, dk=dk)


def _gla_head(q_ref, k_ref, v_ref, r_ref, ag_ref, w2_ref, gb_ref, gn_ref, a_ref, s_ref, b_scr, amat_scr, *, tb, dk):
    scale = dk ** -0.5
    q = q_ref[...].astype(F32) * scale
    k = k_ref[...].astype(F32)
    v = v_ref[...]
    g = _gla_log_decay(ag_ref[...], w2_ref[...], gb_ref[...])
    row = lax.broadcasted_iota(jnp.int32, (tb, tb), 0)
    col = lax.broadcasted_iota(jnp.int32, (tb, tb), 1)
    causal = col <= row
    tril = jnp.where(causal, 1.0, 0.0).astype(BF16)
    g_hi = g.astype(BF16)
    g_lo = (g - g_hi.astype(F32)).astype(BF16)
    b = (jnp.dot(tril, g_hi, preferred_element_type=F32) + jnp.dot(tril, g_lo, preferred_element_type=F32))
    b_scr[...] = b
    b_last = b[tb - 1:tb, :]
    k_hat = (k * jnp.exp(b_last - b)).astype(BF16)

    s_prev = s_ref[...]
    o = jnp.dot((q * jnp.exp(b)).astype(BF16), s_prev.astype(BF16), preferred_element_type=F32)

    lane = lax.broadcasted_iota(jnp.int32, (GLA_SUB, tb), 1)
    lane_h = lax.broadcasted_iota(jnp.int32, (GLA_SUB, LANES), 1)
    row_h = lax.broadcasted_iota(jnp.int32, (GLA_SUB, LANES), 0)
    mild = jnp.min(b_last) >= -GLA_MILD_LOG_DECAY

    @pl.when(mild)
    def _():
        q_end = (q * jnp.exp(b - b_last)).astype(BF16)
        amat_scr[...] = jnp.where(causal, _nt_dot(q_end, k_hat), 0.0)

    def sub_chunk(i, carry):
        base = pl.multiple_of(i * GLA_SUB, GLA_SUB)
        q_i = q_ref[pl.ds(base, GLA_SUB), :].astype(F32) * scale
        k_i = k_ref[pl.ds(base, GLA_SUB), :].astype(F32)
        b_i = b_scr[pl.ds(base, GLA_SUB), :]
        r_i = b_i[0:1, :]
        q_t = (q_i * jnp.exp(b_i - r_i)).astype(BF16)
        k_t = (k_ref[...].astype(F32) * jnp.exp(jnp.minimum(r_i - b_scr[...], 0.0))).astype(BF16)
        p = jnp.where(lane < base, _nt_dot(q_t, k_t), 0.0)
        lane0 = base % LANES
        d = jnp.zeros((GLA_SUB, LANES), F32)
        for s in range(GLA_SUB):
            e = jnp.exp(jnp.minimum(b_i - b_i[s:s + 1, :], 0.0))
            c = jnp.sum(q_i * e * k_i[s:s + 1, :], axis=-1, keepdims=True)
            d = d + jnp.where(lane_h == lane0 + s, jnp.where(row_h >= s, c, 0.0), 0.0)
        halves = [jnp.where(base // LANES == h, d, 0.0) for h in range(tb // LANES)]
        amat_scr[pl.ds(base, GLA_SUB), :] = p + jnp.concatenate(halves, axis=1)
        return carry

    @pl.when(jnp.logical_not(mild))
    def _():
        lax.fori_loop(0, tb // GLA_SUB, sub_chunk, 0)

    o = o + jnp.dot(amat_scr[...].astype(BF16), v, preferred_element_type=F32)

    decay_col = jnp.broadcast_to(jnp.exp(b_last), (LANES, dk)).T
    dv = v.shape[1]
    decay_full = jnp.concatenate([decay_col] * (dv // LANES), axis=1)
    s_ref[...] = s_prev * decay_full + _tn_dot(k_hat, v)

    rgate = r_ref[...].astype(F32)
    a_ref[...] = (_rms(o, gn_ref[...]) * (rgate * _sigmoid(rgate))).astype(a_ref.dtype)


def gla_prompt(qkvr, ag, w2, gb, gnorm, batch, seq, dk, dv):
    heads = GLA_HEADS
    hg = GLA_HEADS_PER_STEP
    assert heads % hg == 0
    tb = min(GLA_BLOCK, seq)
    nblk = seq // tb
    row = lambda b, h, i: b * nblk + i
    wk, wv = hg * dk, hg * dv
    v0, r0 = (2 * heads * dk) // wv, (2 * heads * dk + heads * dv) // wv
    kern = functools.partial(_gla_prompt_kernel, tb=tb, dk=dk, dv=dv, heads=hg)
    return pl.pallas_call(
        kern,
        out_shape=[jax.ShapeDtypeStruct((batch * seq, heads * dv), BF16),
                   jax.ShapeDtypeStruct((batch, heads, dk, dv), F32)],
        grid=(batch, heads // hg, nblk),
        in_specs=[
            pl.BlockSpec((tb, wk), lambda b, h, i: (row(b, h, i), h)),
            pl.BlockSpec((tb, wk), lambda b, h, i: (row(b, h, i), heads // hg + h)),
            pl.BlockSpec((tb, wv), lambda b, h, i: (row(b, h, i), v0 + h)),
            pl.BlockSpec((tb, wv), lambda b, h, i: (row(b, h, i), r0 + h)),
            pl.BlockSpec((tb, LANES), lambda b, h, i: (row(b, h, i), 0)),
            pl.BlockSpec((LANES, wk), lambda b, h, i: (0, h)),
            pl.BlockSpec((1, wk), lambda b, h, i: (0, h)),
            pl.BlockSpec((1, dv), lambda b, h, i: (0, 0)),
        ],
        out_specs=[pl.BlockSpec((tb, wv), lambda b, h, i: (row(b, h, i), h)),
                   pl.BlockSpec((1, hg, dk, dv), lambda b, h, i: (b, h, 0, 0))],
        scratch_shapes=[pltpu.VMEM((hg, tb, dk), F32), pltpu.VMEM((hg, tb, tb), F32)],
        compiler_params=_cparams("parallel", "parallel", "arbitrary"),
        name="gla_prompt",
    )(qkvr, qkvr, qkvr, qkvr, ag, w2, gb.reshape(1, -1), gnorm.reshape(1, -1))


def _gla_step_kernel(q_ref, k_ref, v_ref, r_ref, ag_ref, w2_ref, gb_ref, gn_ref, s0_ref, a_ref, s_ref, *, nb, dk):
    g = _gla_log_decay(ag_ref[...], w2_ref[...], gb_ref[...])
    reps = LANES // nb
    decay = jnp.exp(g)
    decay_t = jnp.concatenate([decay] * reps, axis=0).T
    k = k_ref[...]
    k_t = jnp.concatenate([k] * reps, axis=0).T
    q = q_ref[...] * (dk ** -0.5)
    q_dec = (q * decay).astype(BF16)
    v = v_ref[...]
    rows = lax.broadcasted_iota(jnp.int32, (nb, v.shape[1]), 0)
    o = jnp.sum(q * k, axis=-1, keepdims=True) * v
    for j in range(nb):
        s0 = s0_ref[j, 0]
        s_ref[j, 0] = s0 * decay_t[:, j:j + 1] + k_t[:, j:j + 1] * v[j:j + 1, :]
        o = o + jnp.where(rows == j, jnp.dot(q_dec, s0.astype(BF16), preferred_element_type=F32), 0.0)
    rgate = r_ref[...]
    a_ref[...] = _rms(o, gn_ref[...]) * (rgate * _sigmoid(rgate))


def gla_step(proj, ag, w2, gb, gnorm, s0, dk, dv):
    heads = GLA_HEADS
    batch = proj.shape[0]
    nb = 8
    kern = functools.partial(_gla_step_kernel, nb=nb, dk=dk)
    return pl.pallas_call(
        kern,
        out_shape=[jax.ShapeDtypeStruct((batch, heads * dv), F32),
                   jax.ShapeDtypeStruct((batch, heads, dk, dv), F32)],
        grid=(batch // nb, heads),
        in_specs=[
            pl.BlockSpec((nb, dk), lambda b, h: (b, h)),
            pl.BlockSpec((nb, dk), lambda b, h: (b, heads + h)),
            pl.BlockSpec((nb, dv), lambda b, h: (b, (2 * heads * dk) // dv + h)),
            pl.BlockSpec((nb, dv), lambda b, h: (b, (2 * heads * dk) // dv + heads + h)),
            pl.BlockSpec((nb, LANES), lambda b, h: (b, 0)),
            pl.BlockSpec((LANES, dk), lambda b, h: (0, h)),
            pl.BlockSpec((1, dk), lambda b, h: (0, h)),
            pl.BlockSpec((1, dv), lambda b, h: (0, 0)),
            pl.BlockSpec((nb, 1, dk, dv), lambda b, h: (b, h, 0, 0)),
        ],
        out_specs=[pl.BlockSpec((nb, dv), lambda b, h: (b, h)),
                   pl.BlockSpec((nb, 1, dk, dv), lambda b, h: (b, h, 0, 0))],
        compiler_params=_cparams("parallel", "parallel"),
        name="gla_step",
    )(proj, proj, proj, proj, ag, w2, gb.reshape(1, -1), gnorm.reshape(1, -1), s0)


def _rel_bucket(dist):
    n = jnp.maximum(dist, 0)
    max_exact = REL_BUCKETS // 2
    large = max_exact + (jnp.log(jnp.maximum(n, 1).astype(F32) / max_exact)
                         / math.log(REL_MAX_DIST / max_exact) * (REL_BUCKETS - max_exact)).astype(jnp.int32)
    large = jnp.minimum(large, REL_BUCKETS - 1)
    return jnp.where(n < max_exact, n, large)


def _lane_tile(x, width):
    return jnp.concatenate([x] * (width // LANES), axis=1)


def _diff_prompt_kernel(lam_ref, q_ref, k_ref, v_ref, w_ref, far_ref, dn_ref, o_ref,
                        m_scr, l_scr, corr_scr, acc_scr, bias_scr, s_scr, p_scr, *, tq, dh, out_scale):
    qi = pl.program_id(2)
    h = pl.program_id(1)
    far = far_ref[h]

    @pl.when(qi == 0)
    def _():
        toep = pltpu.roll(jnp.broadcast_to(w_ref[0], (tq, tq)), 0, 1, stride=1, stride_axis=0)
        row = lax.broadcasted_iota(jnp.int32, (tq, tq), 0)
        col = lax.broadcasted_iota(jnp.int32, (tq, tq), 1)
        bias_scr[0] = jnp.where(col <= row, toep, NEG_BIG)
        bias_scr[1] = jnp.where(col > row, toep, far)

    m_scr[...] = jnp.full_like(m_scr, NEG_BIG)
    l_scr[...] = jnp.zeros_like(l_scr)
    acc_scr[...] = jnp.zeros_like(acc_scr)
    q = q_ref[...]

    def visit(kb, bias_rows):
        start = pl.multiple_of(kb * tq, tq)
        kk = k_ref[pl.ds(start, tq), :]
        vv = v_ref[pl.ds(start, tq), :]
        for c in range(2):
            s_scr[c] = _nt_dot(q[:, c * dh:(c + 1) * dh], kk[:, c * dh:(c + 1) * dh])

        def chunk(r, carry):
            rows = pl.ds(r * ATTN_ROWS, ATTN_ROWS)
            bias = bias_rows(rows)
            for c in range(2):
                s = s_scr[c, rows, :] + bias
                m_old = m_scr[c, rows, :]
                m_new = jnp.maximum(m_old, jnp.max(s, axis=-1, keepdims=True))
                corr = jnp.exp2(m_old - m_new)
                p = jnp.exp2(s - _lane_tile(m_new, tq))
                l_scr[c, rows, :] = l_scr[c, rows, :] * corr + jnp.sum(p, axis=-1, keepdims=True)
                corr_scr[c, rows, :] = corr
                m_scr[c, rows, :] = m_new
                p_scr[c, rows, :] = p.astype(BF16)
            return carry

        for r in range(tq // ATTN_ROWS):
            chunk(r, 0)
        for c in range(2):
            acc_scr[c] = (acc_scr[c] * _lane_tile(corr_scr[c], 2 * dh)
                          + jnp.dot(p_scr[c], vv, preferred_element_type=F32))

    def far_block(kb, carry):
        visit(kb, lambda rows: far)
        return carry

    lax.fori_loop(0, jnp.maximum(qi - 1, 0), far_block, 0)

    @pl.when(qi >= 1)
    def _():
        visit(qi - 1, lambda rows: bias_scr[1, rows, :])

    visit(qi, lambda rows: bias_scr[0, rows, :])
    lam = lam_ref[0]
    o = (acc_scr[0] / _lane_tile(l_scr[0], 2 * dh) - lam * (acc_scr[1] / _lane_tile(l_scr[1], 2 * dh)))
    o_ref[...] = (_rms(o, dn_ref[...]) * out_scale).astype(o_ref.dtype)


def diff_attention_prompt(q, k, v, lam, rel_table, diff_norm, batch, seq, lam_init):
    heads = DIFF_HEADS
    dh = q.shape[1] // (2 * heads)
    tq = min(ATTN_BLOCK, seq)
    assert tq >= REL_MAX_DIST or seq == tq
    nq = seq // tq
    dist = (tq - jnp.arange(tq, dtype=jnp.int32)) % tq
    w = (rel_table[_rel_bucket(dist)].astype(F32) * LOG2E).T.reshape(heads, 1, tq)
    far = rel_table[REL_BUCKETS - 1].astype(F32) * LOG2E
    kern = functools.partial(_diff_prompt_kernel, tq=tq, dh=dh, out_scale=1.0 - lam_init)
    return pl.pallas_call(
        kern,
        out_shape=jax.ShapeDtypeStruct((batch * seq, heads * 2 * dh), BF16),
        grid=(batch, heads, nq),
        in_specs=[
            pl.BlockSpec(memory_space=pltpu.SMEM),
            pl.BlockSpec((tq, 2 * dh), lambda b, h, i: (b * nq + i, h)),
            pl.BlockSpec((seq, 2 * dh), lambda b, h, i: (b, h)),
            pl.BlockSpec((seq, 2 * dh), lambda b, h, i: (b, h)),
            pl.BlockSpec((1, 1, tq), lambda b, h, i: (h, 0, 0)),
            pl.BlockSpec(memory_space=pltpu.SMEM),
            pl.BlockSpec((1, 2 * dh), lambda b, h, i: (0, 0)),
        ],
        out_specs=pl.BlockSpec((tq, 2 * dh), lambda b, h, i: (b * nq + i, h)),
        scratch_shapes=[pltpu.VMEM((2, tq, LANES), F32), pltpu.VMEM((2, tq, LANES), F32),
                        pltpu.VMEM((2, tq, LANES), F32), pltpu.VMEM((2, tq, 2 * dh), F32),
                        pltpu.VMEM((2, tq, tq), F32), pltpu.VMEM((2, tq, tq), F32), pltpu.VMEM((2, tq, tq), BF16)],
        compiler_params=_cparams("arbitrary", "arbitrary", "arbitrary"),
        name="diff_attention_prompt",
    )(lam.reshape(1), q, k, v, w, far, diff_norm.reshape(1, -1))


def _diff_paged_kernel(pt_ref, lam_ref, q_ref, *rest, n_pages, pages, out_scale, heads):
    k_refs = rest[:pages]
    v_refs = rest[pages:2 * pages]
    bias_ref, kn_ref, vn_ref, bn_ref, dn_ref, o_ref, m_scr, l_scr, acc_scr = rest[2 * pages:]
    del pt_ref
    jg = pl.program_id(1)

    @pl.when(jg == 0)
    def _():
        m_scr[...] = jnp.full_like(m_scr, NEG_BIG)
        l_scr[...] = jnp.zeros_like(l_scr)
        acc_scr[...] = jnp.zeros_like(acc_scr)

    q = q_ref[0]

    def update(ks, vs, biases):
        ss = [_nt_dot(q, kk.astype(BF16)) + bias for kk, bias in zip(ks, biases)]
        m_old = m_scr[...]
        m_new = m_old
        for s in ss:
            m_new = jnp.maximum(m_new, jnp.max(s, axis=-1, keepdims=True))
        corr = jnp.exp(m_old - m_new)
        l_new = l_scr[...] * corr
        acc = acc_scr[...] * corr
        for s, vv in zip(ss, vs):
            p = jnp.exp(s - m_new)
            l_new = l_new + jnp.sum(p, axis=-1, keepdims=True)
            acc = acc + jnp.dot(p.astype(BF16), vv.astype(BF16), preferred_element_type=F32)
        l_scr[...] = l_new
        acc_scr[...] = acc
        m_scr[...] = m_new

    def as_rows(ref):
        x = ref[0, 0]
        return x.reshape(x.shape[0] * x.shape[1], x.shape[2])

    biases = [bias_ref[(jg * pages + i == n_pages - 1).astype(jnp.int32)] for i in range(pages)]
    update([as_rows(r) for r in k_refs], [as_rows(r) for r in v_refs], biases)

    @pl.when(jg == pl.num_programs(1) - 1)
    def _():
        update([kn_ref[0]], [vn_ref[0]], [bn_ref[...]])
        o = acc_scr[...] / l_scr[...]
        out = o[:heads] - lam_ref[0] * o[heads:]
        o_ref[0] = _rms(out, dn_ref[...]) * out_scale


def diff_attention_paged(q, k_new, v_new, lam, cache_k, cache_v, layer, page_table, rel_table, diff_norm, lam_init):
    heads = DIFF_HEADS
    batch = q.shape[0]
    dh = q.shape[1] // (2 * heads)
    n_pages = page_table.shape[1]
    page = cache_k.shape[2]
    past = n_pages * page
    pages = math.gcd(PAGES_PER_STEP, n_pages)
    assert page >= REL_MAX_DIST
    rows = page * heads
    q4 = (q.reshape(batch, heads, 2, dh) * (dh ** -0.5)).astype(BF16)
    zeros = jnp.zeros_like(q4[:, :, 0])
    q_mat = jnp.concatenate([jnp.concatenate([q4[:, :, 0], zeros], axis=-1),
                             jnp.concatenate([zeros, q4[:, :, 1]], axis=-1)], axis=1)
    tok = jnp.arange(page, dtype=jnp.int32)
    dist_last = past - ((n_pages - 1) * page + tok)
    b_far = jnp.broadcast_to(rel_table[REL_BUCKETS - 1].astype(F32)[:, None], (heads, page))
    b_last = rel_table[_rel_bucket(dist_last)].astype(F32).T
    own = (jnp.arange(heads)[:, None, None] == jnp.arange(heads)[None, None, :])
    def expand(bh):
        t = jnp.where(own, bh[:, :, None], NEG_BIG).reshape(heads, rows)
        return jnp.concatenate([t, t], axis=0)
    bias = jnp.stack([expand(b_far), expand(b_last)])
    pad = LANES - heads
    kn = jnp.pad(k_new.reshape(batch, heads, 2 * dh), ((0, 0), (0, pad), (0, 0)))
    vn = jnp.pad(v_new.reshape(batch, heads, 2 * dh), ((0, 0), (0, pad), (0, 0)))
    b0 = jnp.where(jnp.arange(heads)[:, None] == jnp.arange(LANES)[None, :],
                   rel_table[0].astype(F32)[:, None], NEG_BIG)
    bn = jnp.concatenate([b0, b0], axis=0)

    def page_spec(i):
        return pl.BlockSpec((1, 1) + cache_k.shape[2:],
                            lambda b, j, pt: (layer, pt[b * n_pages + j * pages + i], 0, 0, 0))

    kern = functools.partial(_diff_paged_kernel, n_pages=n_pages, pages=pages, out_scale=1.0 - lam_init, heads=heads)
    out = pl.pallas_call(
        kern,
        out_shape=jax.ShapeDtypeStruct((batch, heads, 2 * dh), F32),
        grid_spec=pltpu.PrefetchScalarGridSpec(
            num_scalar_prefetch=1,
            grid=(batch, n_pages // pages),
            in_specs=[pl.BlockSpec(memory_space=pltpu.SMEM),
                      pl.BlockSpec((1, 2 * heads, 2 * dh), lambda b, j, pt: (b, 0, 0))]
                     + [page_spec(i) for i in range(pages)] + [page_spec(i) for i in range(pages)]
                     + [pl.BlockSpec((2, 2 * heads, rows), lambda b, j, pt: (0, 0, 0)),
                        pl.BlockSpec((1, LANES, 2 * dh), lambda b, j, pt: (b, 0, 0)),
                        pl.BlockSpec((1, LANES, 2 * dh), lambda b, j, pt: (b, 0, 0)),
                        pl.BlockSpec((2 * heads, LANES), lambda b, j, pt: (0, 0)),
                        pl.BlockSpec((1, 2 * dh), lambda b, j, pt: (0, 0))],
            out_specs=pl.BlockSpec((1, heads, 2 * dh), lambda b, j, pt: (b, 0, 0)),
            scratch_shapes=[pltpu.VMEM((2 * heads, 1), F32), pltpu.VMEM((2 * heads, 1), F32),
                            pltpu.VMEM((2 * heads, 2 * dh), F32)],
        ),
        compiler_params=_cparams("parallel", "arbitrary"),
        name="diff_attention_paged",
    )(page_table.reshape(-1), lam.reshape(1), q_mat, *([cache_k] * pages), *([cache_v] * pages),
      bias, kn, vn, bn, diff_norm.reshape(1, -1))
    return out.reshape(batch, heads * 2 * dh)


def _mem_attn_kernel(q_ref, k_ref, v_ref, o_ref, *, scale):
    q = q_ref[0]
    tq = q.shape[0]
    if tq < 8:
        q = jnp.broadcast_to(q[0:1], (8, q.shape[1]))
    s = _nt_dot(q.astype(BF16), k_ref[0].astype(BF16)) * scale
    p = jnp.exp(s - jnp.max(s, axis=-1, keepdims=True))
    o = jnp.dot(p.astype(BF16), v_ref[0].astype(BF16), preferred_element_type=F32)
    o = o / jnp.sum(p, axis=-1, keepdims=True)
    o_ref[0] = o[:tq].astype(o_ref.dtype)


def memory_attention(q3, col0, mem_k, mem_v, out_dtype, tq=2048):
    batch, seq, _ = q3.shape
    mtok, width = mem_k.shape[1:]
    md = width // MEM_HEADS
    tq = min(tq, seq)
    kern = functools.partial(_mem_attn_kernel, scale=md ** -0.5)
    return pl.pallas_call(
        kern,
        out_shape=jax.ShapeDtypeStruct((batch, seq, width), out_dtype),
        grid=(batch, MEM_HEADS, seq // tq),
        in_specs=[pl.BlockSpec((1, tq, md), lambda b, h, i: (b, i, col0 + h)),
                  pl.BlockSpec((1, mtok, md), lambda b, h, i: (b, 0, h)),
                  pl.BlockSpec((1, mtok, md), lambda b, h, i: (b, 0, h))],
        out_specs=pl.BlockSpec((1, tq, md), lambda b, h, i: (b, i, h)),
        compiler_params=_cparams("parallel", "parallel", "parallel"),
        name="memory_attention",
    )(q3, mem_k, mem_v)


def _mem_step_kernel(q_ref, k_ref, v_ref, o_ref, *, scale, heads):
    kk = k_ref[0, 0]
    rows = kk.shape[0] * kk.shape[1]
    k2 = kk.reshape(rows, kk.shape[2]).astype(BF16)
    v2 = v_ref[0, 0].reshape(rows, kk.shape[2]).astype(BF16)
    s = _nt_dot(q_ref[0].astype(BF16), k2) * scale
    lane = lax.broadcasted_iota(jnp.int32, s.shape, 1)
    qrow = lax.broadcasted_iota(jnp.int32, s.shape, 0)
    s = jnp.where(lane % heads == qrow, s, NEG_BIG)
    p = jnp.exp(s - jnp.max(s, axis=-1, keepdims=True))
    o = jnp.dot(p.astype(BF16), v2, preferred_element_type=F32)
    o_ref[0] = o / jnp.sum(p, axis=-1, keepdims=True)


def memory_attention_step(q, cache_k, cache_v, layer):
    batch = q.shape[0]
    _, _, mtok, heads, md = cache_k.shape
    qrows = 8
    assert heads <= qrows
    q8 = jnp.pad(q.reshape(batch, heads, md), ((0, 0), (0, qrows - heads), (0, 0)))
    kern = functools.partial(_mem_step_kernel, scale=md ** -0.5, heads=heads)
    cache_spec = pl.BlockSpec((1, 1, mtok, heads, md), lambda b: (layer, b, 0, 0, 0))
    out = pl.pallas_call(
        kern,
        out_shape=jax.ShapeDtypeStruct((batch, qrows, md), F32),
        grid=(batch,),
        in_specs=[pl.BlockSpec((1, qrows, md), lambda b: (b, 0, 0)), cache_spec, cache_spec],
        out_specs=pl.BlockSpec((1, qrows, md), lambda b: (b, 0, 0)),
        compiler_params=_cparams("parallel"),
        name="memory_attention_step",
    )(q8, cache_k, cache_v)
    return out[:, :heads].reshape(batch, heads * md)


def _merge_kernel(a_ref, b_ref, c_ref, g0_ref, g1_ref, g2_ref, x_ref, w_ref, npost_ref, npre_ref, x1_ref, h2_ref):
    tm = x_ref.shape[0]
    sub = math.gcd(tm, MERGE_SUB_ROWS)
    for rb in range(tm // sub):
        rs = pl.ds(rb * sub, sub)
        merged = 0.5 * ((jnp.tanh(0.5 * g0_ref[rs, :].astype(F32)) + 1.0) * a_ref[rs, :].astype(F32)
                        + (jnp.tanh(0.5 * g1_ref[rs, :].astype(F32)) + 1.0) * b_ref[rs, :].astype(F32)
                        + (jnp.tanh(0.5 * g2_ref[rs, :].astype(F32)) + 1.0) * c_ref[rs, :].astype(F32))
        y = jnp.dot(merged.astype(BF16), w_ref[...], preferred_element_type=F32)
        x1 = x_ref[rs, :] + _rms(y, npost_ref[...])
        x1_ref[rs, :] = x1
        h2_ref[rs, :] = _rms(x1, npre_ref[...]).astype(h2_ref.dtype)


def merge_out_proj(a, b, c, gates_arr, gate_col0, x, w_out, norm_post, norm_pre_ffn, tm=256):
    m, d = x.shape
    tm = min(tm, m)
    rowblk = lambda i: (i, 0)
    vec = pl.BlockSpec((1, d), lambda i: (0, 0))
    return pl.pallas_call(
        _merge_kernel,
        out_shape=[jax.ShapeDtypeStruct((m, d), F32), jax.ShapeDtypeStruct((m, d), BF16)],
        grid=(m // tm,),
        in_specs=[pl.BlockSpec((tm, d), rowblk), pl.BlockSpec((tm, d), rowblk), pl.BlockSpec((tm, d), rowblk),
                  pl.BlockSpec((tm, d), lambda i: (i, gate_col0)),
                  pl.BlockSpec((tm, d), lambda i: (i, gate_col0 + 1)),
                  pl.BlockSpec((tm, d), lambda i: (i, gate_col0 + 2)),
                  pl.BlockSpec((tm, d), rowblk),
                  pl.BlockSpec((d, d), lambda i: (0, 0)),
                  vec, vec],
        out_specs=[pl.BlockSpec((tm, d), rowblk), pl.BlockSpec((tm, d), rowblk)],
        compiler_params=_cparams("parallel"),
        name="merge_out_proj",
    )(a, b, c, gates_arr, gates_arr, gates_arr, x, w_out, norm_post.reshape(1, d), norm_pre_ffn.reshape(1, d))


def _gelu_tanh(x):
    return 0.5 * x * (1.0 + jnp.tanh(math.sqrt(2.0 / math.pi) * (x + 0.044715 * (x * x * x))))


def _ffn_up_seq_kernel(h_ref, wg_ref, wv_ref, cwg_ref, cwv_ref, cbg_ref, cbv_ref, stg_ref, stv_ref,
                       act_ref, tailg_ref, tailv_ref, carryg, carryv, *, tiles_per_seq):
    mi = pl.program_id(1)
    h = h_ref[...]
    tm = h.shape[0]
    row = lax.broadcasted_iota(jnp.int32, (tm, 1), 0)
    seq_start = mi % tiles_per_seq == 0

    def conv(w_ref, cw_ref, cb_ref, st_ref, carry, tail_ref):
        u = jnp.dot(h, w_ref[...], preferred_element_type=F32)
        st = st_ref[0]
        prev2 = jnp.where(seq_start, st[0:1, :], carry[6:7, :])
        prev1 = jnp.where(seq_start, st[1:2, :], carry[7:8, :])
        u1 = jnp.where(row >= 1, pltpu.roll(u, 1, 0), prev1)
        u2 = jnp.where(row >= 2, pltpu.roll(u, 2, 0), jnp.where(row == 1, prev1, prev2))
        cw = cw_ref[...]
        tail = u[tm - 8:tm, :]
        carry[...] = tail
        tail_ref[0] = tail
        return cb_ref[...] + u2 * cw[0:1, :] + u1 * cw[1:2, :] + u * cw[2:3, :]

    gate = conv(wg_ref, cwg_ref, cbg_ref, stg_ref, carryg, tailg_ref)
    val = conv(wv_ref, cwv_ref, cbv_ref, stv_ref, carryv, tailv_ref)
    act_ref[...] = (_gelu_tanh(gate) * val).astype(act_ref.dtype)


def ffn_up_seq(h2, w_up, conv_w, conv_b, conv_state, seq, tm=1024, tn=512):
    m, d = h2.shape
    f = w_up.shape[1] // 2
    tm = min(tm, seq)
    tiles_per_seq = seq // tm
    nj = f // tn
    kern = functools.partial(_ffn_up_seq_kernel, tiles_per_seq=tiles_per_seq)
    cw = conv_w.astype(F32)
    cb = conv_b.reshape(1, -1).astype(F32)
    return pl.pallas_call(
        kern,
        out_shape=[jax.ShapeDtypeStruct((m, f), BF16),
                   jax.ShapeDtypeStruct((m // tm, 8, f), F32),
                   jax.ShapeDtypeStruct((m // tm, 8, f), F32)],
        grid=(nj, m // tm),
        in_specs=[pl.BlockSpec((tm, d), lambda j, i: (i, 0)),
                  pl.BlockSpec((d, tn), lambda j, i: (0, j)),
                  pl.BlockSpec((d, tn), lambda j, i: (0, nj + j)),
                  pl.BlockSpec((cw.shape[0], tn), lambda j, i: (0, j)),
                  pl.BlockSpec((cw.shape[0], tn), lambda j, i: (0, nj + j)),
                  pl.BlockSpec((1, tn), lambda j, i: (0, j)),
                  pl.BlockSpec((1, tn), lambda j, i: (0, nj + j)),
                  pl.BlockSpec((1, 2, tn), lambda j, i: (i // tiles_per_seq, 0, j)),
                  pl.BlockSpec((1, 2, tn), lambda j, i: (i // tiles_per_seq, 0, nj + j))],
        out_specs=[pl.BlockSpec((tm, tn), lambda j, i: (i, j)),
                   pl.BlockSpec((1, 8, tn), lambda j, i: (i, 0, j)),
                   pl.BlockSpec((1, 8, tn), lambda j, i: (i, 0, j))],
        scratch_shapes=[pltpu.VMEM((8, tn), F32), pltpu.VMEM((8, tn), F32)],
        compiler_params=_cparams("parallel", "arbitrary"),
        name="ffn_up_seq",
    )(h2, w_up, w_up, cw, cw, cb, cb, conv_state, conv_state)


def _ffn_up_step_kernel(h_ref, wg_ref, wv_ref, cwg_ref, cwv_ref, cbg_ref, cbv_ref, stg_ref, stv_ref,
                        act_ref, ug_ref, uv_ref):
    h = h_ref[...]

    def conv(w_ref, cw_ref, cb_ref, st_ref, u_ref):
        u = jnp.dot(h, w_ref[...], preferred_element_type=F32)
        u_ref[...] = u
        cw = cw_ref[...]
        return cb_ref[...] + st_ref[0] * cw[0:1, :] + st_ref[1] * cw[1:2, :] + u * cw[2:3, :]

    gate = conv(wg_ref, cwg_ref, cbg_ref, stg_ref, ug_ref)
    val = conv(wv_ref, cwv_ref, cbv_ref, stv_ref, uv_ref)
    act_ref[...] = (_gelu_tanh(gate) * val).astype(act_ref.dtype)


def ffn_up_step(h2, w_up, conv_w, conv_b, conv_state_t, tn=512):
    m, d = h2.shape
    f = w_up.shape[1] // 2
    nj = f // tn
    cw = conv_w.astype(F32)
    cb = conv_b.reshape(1, -1).astype(F32)
    return pl.pallas_call(
        _ffn_up_step_kernel,
        out_shape=[jax.ShapeDtypeStruct((m, f), BF16), jax.ShapeDtypeStruct((m, f), F32),
                   jax.ShapeDtypeStruct((m, f), F32)],
        grid=(nj,),
        in_specs=[pl.BlockSpec((m, d), lambda j: (0, 0)),
                  pl.BlockSpec((d, tn), lambda j: (0, j)),
                  pl.BlockSpec((d, tn), lambda j: (0, nj + j)),
                  pl.BlockSpec((cw.shape[0], tn), lambda j: (0, j)),
                  pl.BlockSpec((cw.shape[0], tn), lambda j: (0, nj + j)),
                  pl.BlockSpec((1, tn), lambda j: (0, j)),
                  pl.BlockSpec((1, tn), lambda j: (0, nj + j)),
                  pl.BlockSpec((2, m, tn), lambda j: (0, 0, j)),
                  pl.BlockSpec((2, m, tn), lambda j: (0, 0, nj + j))],
        out_specs=[pl.BlockSpec((m, tn), lambda j: (0, j)),
                   pl.BlockSpec((m, tn), lambda j: (0, j)),
                   pl.BlockSpec((m, tn), lambda j: (0, j))],
        compiler_params=_cparams("parallel"),
        name="ffn_up_step",
    )(h2, w_up, w_up, cw, cw, cb, cb, conv_state_t, conv_state_t)


def _ffn_down_kernel(act_ref, w_ref, x_ref, n_ref, y_ref):
    f = jnp.dot(act_ref[...], w_ref[...], preferred_element_type=F32)
    y_ref[...] = x_ref[...] + _rms(f, n_ref[...])


def ffn_down(act, w_down, x1, norm_post, tm=512):
    m, f = act.shape
    d = w_down.shape[1]
    tm = min(tm, m)
    assert m % tm == 0
    return pl.pallas_call(
        _ffn_down_kernel,
        out_shape=jax.ShapeDtypeStruct((m, d), F32),
        grid=(m // tm,),
        in_specs=[pl.BlockSpec((tm, f), lambda i: (i, 0)),
                  pl.BlockSpec((f, d), lambda i: (0, 0), pipeline_mode=pl.Buffered(1)),
                  pl.BlockSpec((tm, d), lambda i: (i, 0)),
                  pl.BlockSpec((1, d), lambda i: (0, 0))],
        out_specs=pl.BlockSpec((tm, d), lambda i: (i, 0)),
        compiler_params=_cparams("parallel"),
        name="ffn_down",
    )(act, w_down, x1, norm_post.reshape(1, d))


def kernel(x_prompt, x_sample, cache_k, cache_v, state_gla, cache_mem_k, cache_mem_v, state_ffn_conv, page_table, mem_prompt, rel_bias_table, norm_pre_mix, norm_post_mix, norm_pre_ffn, norm_post_ffn, norm_mem, w_in, w_gla_gate2, b_gla_gate, gla_norm, diff_lambda_q1, diff_lambda_k1, diff_lambda_q2, diff_lambda_k2, diff_norm, w_mem_kv, w_out, w_up, ffn_conv_w, ffn_conv_b, w_down):
    depth = w_in.shape[0]
    bp, tp, d = x_prompt.shape
    bs, ts, _ = x_sample.shape
    assert ts == 1
    dk, dv = d // (2 * GLA_HEADS), d // GLA_HEADS
    mtok = mem_prompt.shape[1]
    f2 = w_up.shape[2]

    yp = x_prompt.reshape(bp * tp, d)
    ys = x_sample.reshape(bs * ts, d)
    outs = {n: [] for n in ("kp", "vp", "gp", "mk", "mv", "cp", "ks", "vs", "gs", "cs")}
    for l in range(depth):
        lam_init = 0.8 - 0.6 * math.exp(-0.3 * l)
        lam = (jnp.exp(jnp.sum(diff_lambda_q1[l].astype(F32) * diff_lambda_k1[l].astype(F32)))
               - jnp.exp(jnp.sum(diff_lambda_q2[l].astype(F32) * diff_lambda_k2[l].astype(F32))) + lam_init)
        rank = w_gla_gate2.shape[1]
        w2 = jnp.pad(w_gla_gate2[l].astype(F32), ((0, LANES - rank), (0, 0)))
        c_ag = 3 * d
        c_qd = c_ag + rank
        c_kd, c_vd, c_qm = c_qd + d, c_qd + 2 * d, c_qd + 3 * d
        q_scale = (d // (2 * DIFF_HEADS)) ** -0.5 * LOG2E
        w_out_b, w_up_b, w_down_b = w_out[l].astype(BF16), w_up[l].astype(BF16), w_down[l].astype(BF16)
        w_in_t = jnp.swapaxes(w_in, 1, 2)
        proj = functools.partial(in_proj, w3=w_in_t, layer=l, by_rows=True)

        hs = rmsnorm_cast(ys, norm_pre_mix[l])
        (qkvr_s,) = proj(hs, col0=0, n=3 * d, out_dtypes=[F32])
        (ag_s,) = proj(hs, col0=c_ag, n=LANES, out_dtypes=[F32])
        (rest_s,) = proj(hs, col0=c_qd, n=7 * d, out_dtypes=[F32])
        kd_s, vd_s = rest_s[:, d:2 * d], rest_s[:, 2 * d:3 * d]
        a_s, gla_state_s = gla_step(qkvr_s, ag_s, w2, b_gla_gate[l], gla_norm[l], state_gla[l], dk, dv)
        b_s = diff_attention_paged(rest_s[:, :d], kd_s, vd_s, lam, cache_k, cache_v, l, page_table,
                                   rel_bias_table, diff_norm[l], lam_init)
        c_s = memory_attention_step(rest_s[:, 3 * d:4 * d], cache_mem_k, cache_mem_v, l)
        x1_s, h2_s = merge_out_proj(a_s, b_s, c_s, rest_s, 4, ys, w_out_b,
                                    norm_post_mix[l], norm_pre_ffn[l])
        st = state_ffn_conv[l].astype(F32)
        act_s, ug_s, uv_s = ffn_up_step(h2_s, w_up_b, ffn_conv_w[l], ffn_conv_b[l], jnp.swapaxes(st, 0, 1))
        ys = ffn_down(act_s, w_down_b, x1_s, norm_post_ffn[l])
        u_s = jnp.concatenate([ug_s, uv_s], axis=-1)
        outs["ks"].append(kd_s.reshape(bs, ts, DIFF_HEADS, -1))
        outs["vs"].append(vd_s.reshape(bs, ts, DIFF_HEADS, -1))
        outs["gs"].append(gla_state_s)
        outs["cs"].append(jnp.stack([st[:, 1], u_s], axis=1))

        hm = rmsnorm_cast(mem_prompt.reshape(bp * mtok, d), norm_mem[l])
        (mem_k,) = in_proj(hm, w_mem_kv, l, 0, d, [F32])
        (mem_v,) = in_proj(hm, w_mem_kv, l, d, d, [F32])
        h = rmsnorm_cast(yp, norm_pre_mix[l])
        (qkvr,) = proj(h, col0=0, n=3 * d, out_dtypes=[BF16])
        (ag,) = proj(h, col0=c_ag, n=LANES, out_dtypes=[F32])
        (qd,) = proj(h, col0=c_qd, n=d, out_dtypes=[BF16], out_scale=q_scale)
        k_rows, kd_b = proj(h, col0=c_kd, n=d, out_dtypes=[F32, BF16])
        v_rows, vd_b = proj(h, col0=c_vd, n=d, out_dtypes=[F32, BF16])
        (qm_gates,) = proj(h, col0=c_qm, n=4 * d, out_dtypes=[BF16])
        a, gla_state = gla_prompt(qkvr, ag, w2, b_gla_gate[l], gla_norm[l], bp, tp, dk, dv)
        b = diff_attention_prompt(qd, kd_b, vd_b, lam, rel_bias_table, diff_norm[l], bp, tp, lam_init)
        c = memory_attention(qm_gates.reshape(bp, tp, -1), 0, mem_k.reshape(bp, mtok, d),
                             mem_v.reshape(bp, mtok, d), BF16)
        x1, h2 = merge_out_proj(a, b, c.reshape(bp * tp, d), qm_gates, 1, yp, w_out_b, norm_post_mix[l],
                                norm_pre_ffn[l])
        conv0 = jnp.zeros((bp, ffn_conv_w.shape[1] - 1, f2), F32)
        act, tail_g, tail_v = ffn_up_seq(h2, w_up_b, ffn_conv_w[l], ffn_conv_b[l], conv0, tp)
        yp = ffn_down(act, w_down_b, x1, norm_post_ffn[l])
        tiles = tail_g.shape[0] // bp
        tail = jnp.concatenate([tail_g, tail_v], axis=-1).reshape(bp, tiles, 8, f2)
        outs["kp"].append(k_rows.reshape(bp, tp, DIFF_HEADS, -1))
        outs["vp"].append(v_rows.reshape(bp, tp, DIFF_HEADS, -1))
        outs["gp"].append(gla_state)
        outs["mk"].append(mem_k.reshape(bp, mtok, MEM_HEADS, -1))
        outs["mv"].append(mem_v.reshape(bp, mtok, MEM_HEADS, -1))
        outs["cp"].append(tail[:, -1, 6:8])

    st = lambda n: jnp.stack(outs[n])
    return (yp.reshape(bp, tp, d), ys.reshape(bs, ts, d), st("kp"), st("vp"), st("gp"), st("mk"), st("mv"),
            st("cp"), st("ks"), st("vs"), st("gs"), st("cs"))
```

```python
import functools
import math

import jax
import jax.numpy as jnp
from jax import lax
from jax.experimental import pallas as pl
from jax.experimental.pallas import tpu as pltpu

F32 = jnp.float32
BF16 = jnp.bfloat16

GLA_HEADS = 4
GLA_GATE_TAU = 16.0
DIFF_HEADS = 8
MEM_HEADS = 4
REL_BUCKETS = 32
REL_MAX_DIST = 128
N_BRANCH = 3
EPS = 1e-6

NEG_BIG = -1e30
LOG2E = math.log2(math.e)
LANES = 128
VMEM_LIMIT_V7X = 56 * 1024 * 1024
GLA_BLOCK = 256
GLA_HEADS_PER_STEP = 2
GLA_SUB = 16
GLA_MILD_LOG_DECAY = 60.0
ATTN_BLOCK = 512
ATTN_ROWS = 64
PAGES_PER_STEP = 8
MERGE_SUB_ROWS = 128


def _cparams(*sem):
    return pltpu.CompilerParams(dimension_semantics=sem, vmem_limit_bytes=VMEM_LIMIT_V7X)


def _nt_dot(a, b):
    return lax.dot_general(a, b, (((1,), (1,)), ((), ())), preferred_element_type=F32)


def _tn_dot(a, b):
    return lax.dot_general(a, b, (((0,), (0,)), ((), ())), preferred_element_type=F32)


def _rms(x, g):
    return x * lax.rsqrt(jnp.mean(x * x, axis=-1, keepdims=True) + EPS) * g


def _sigmoid(x):
    return 0.5 * jnp.tanh(0.5 * x) + 0.5


def _norm_kernel(x_ref, g_ref, o_ref):
    o_ref[...] = _rms(x_ref[...], g_ref[...]).astype(o_ref.dtype)


def rmsnorm_cast(x, g, tm=512):
    m, d = x.shape
    tm = min(tm, m)
    assert m % tm == 0
    return pl.pallas_call(
        _norm_kernel,
        out_shape=jax.ShapeDtypeStruct((m, d), BF16),
        grid=(m // tm,),
        in_specs=[pl.BlockSpec((tm, d), lambda i: (i, 0)), pl.BlockSpec((1, d), lambda i: (0, 0))],
        out_specs=pl.BlockSpec((tm, d), lambda i: (i, 0)),
        compiler_params=_cparams("parallel"),
        name="rmsnorm_cast",
    )(x, g.reshape(1, d))


def _proj_kernel(x_ref, w_ref, *rest, shift, n_out, out_scale, by_rows):
    if shift:
        wn_ref, rest = rest[0], rest[1:]
    o_refs, wb_scr = rest[:n_out], rest[n_out]

    @pl.when(pl.program_id(1) == 0)
    def _():
        if shift:
            wb_scr[...] = jnp.concatenate([w_ref[0, shift:, :], wn_ref[0]], axis=0).astype(BF16)
        else:
            wb_scr[...] = w_ref[0].astype(BF16)

    acc = _nt_dot(x_ref[...], wb_scr[...]) if by_rows else jnp.dot(x_ref[...], wb_scr[...],
                                                                  preferred_element_type=F32)
    if out_scale != 1.0:
        acc = acc * out_scale
    for o_ref in o_refs:
        o_ref[...] = acc.astype(o_ref.dtype)


def in_proj(x, w3, layer, col0, n, out_dtypes, out_scale=1.0, by_rows=False, tm=1024, tn=1024):
    m, k = x.shape
    tm, tn = min(tm, m), min(tn, n)
    base, shift = col0 - col0 % tn, col0 % tn
    assert m % tm == 0 and n % tn == 0 and shift % 8 == 0 and tn % max(shift, 1) == 0
    assert by_rows or shift == 0
    jb = base // tn
    w_block = (1, tn, k) if by_rows else (1, k, tn)
    w_index = (lambda j, i: (layer, jb + j, 0)) if by_rows else (lambda j, i: (layer, 0, jb + j))
    in_specs = [pl.BlockSpec((tm, k), lambda j, i: (i, 0)), pl.BlockSpec(w_block, w_index)]
    args = [x, w3]
    if shift:
        in_specs.append(pl.BlockSpec((1, shift, k), lambda j, i: (layer, (jb + j + 1) * (tn // shift), 0)))
        args.append(w3)
    kern = functools.partial(_proj_kernel, shift=shift, n_out=len(out_dtypes), out_scale=out_scale, by_rows=by_rows)
    return pl.pallas_call(
        kern,
        out_shape=[jax.ShapeDtypeStruct((m, n), dt) for dt in out_dtypes],
        grid=(n // tn, m // tm),
        in_specs=in_specs,
        out_specs=[pl.BlockSpec((tm, tn), lambda j, i: (i, j)) for _ in out_dtypes],
        scratch_shapes=[pltpu.VMEM((tn, k) if by_rows else (k, tn), BF16)],
        compiler_params=_cparams("parallel", "arbitrary"),
        name="in_proj",
    )(*args)


def _log_sigmoid(z):
    return jnp.minimum(z, 0.0) - jnp.log(1.0 + jnp.exp(-jnp.abs(z)))


def _gla_log_decay(ag, w2, gb):
    z = jnp.dot(ag.astype(BF16), w2.astype(BF16), preferred_element_type=F32)
    return _log_sigmoid(z + gb) * (1.0 / GLA_GATE_TAU)


def _gla_prompt_kernel(q_ref, k_ref, v_ref, r_ref, ag_ref, w2_ref, gb_ref, gn_ref, a_ref, s_ref,
                       b_scr, amat_scr, *, tb, dk, dv, heads):
    @pl.when(pl.program_id(2) == 0)
    def _():
        s_ref[...] = jnp.zeros_like(s_ref)

    for hh in range(heads):
        lk = pl.ds(hh * dk, dk)
        lv = pl.ds(hh * dv, dv)
        _gla_head(q_ref.at[:, lk], k_ref.at[:, lk], v_ref.at[:, lv], r_ref.at[:, lv], ag_ref, w2_ref.at[:, lk],
                  gb_ref.at[:, lk], gn_ref, a_ref.at[:, lv], s_ref.at[0, hh], b_scr.at[hh], amat_scr.at[hh],
                  tb=tb, dk=dk)


def _gla_head(q_ref, k_ref, v_ref, r_ref, ag_ref, w2_ref, gb_ref, gn_ref, a_ref, s_ref, b_scr, amat_scr, *, tb, dk):
    scale = dk ** -0.5
    q = q_ref[...].astype(F32) * scale
    k = k_ref[...].astype(F32)
    v = v_ref[...]
    g = _gla_log_decay(ag_ref[...], w2_ref[...], gb_ref[...])
    row = lax.broadcasted_iota(jnp.int32, (tb, tb), 0)
    col = lax.broadcasted_iota(jnp.int32, (tb, tb), 1)
    causal = col <= row
    tril = jnp.where(causal, 1.0, 0.0).astype(BF16)
    g_hi = g.astype(BF16)
    g_lo = (g - g_hi.astype(F32)).astype(BF16)
    b = (jnp.dot(tril, g_hi, preferred_element_type=F32) + jnp.dot(tril, g_lo, preferred_element_type=F32))
    b_scr[...] = b
    b_last = b[tb - 1:tb, :]
    k_hat = (k * jnp.exp(b_last - b)).astype(BF16)

    s_prev = s_ref[...]
    o = jnp.dot((q * jnp.exp(b)).astype(BF16), s_prev.astype(BF16), preferred_element_type=F32)

    lane = lax.broadcasted_iota(jnp.int32, (GLA_SUB, tb), 1)
    lane_h = lax.broadcasted_iota(jnp.int32, (GLA_SUB, LANES), 1)
    row_h = lax.broadcasted_iota(jnp.int32, (GLA_SUB, LANES), 0)
    mild = jnp.min(b_last) >= -GLA_MILD_LOG_DECAY

    @pl.when(mild)
    def _():
        q_end = (q * jnp.exp(b - b_last)).astype(BF16)
        amat_scr[...] = jnp.where(causal, _nt_dot(q_end, k_hat), 0.0)

    def sub_chunk(i, carry):
        base = pl.multiple_of(i * GLA_SUB, GLA_SUB)
        q_i = q_ref[pl.ds(base, GLA_SUB), :].astype(F32) * scale
        k_i = k_ref[pl.ds(base, GLA_SUB), :].astype(F32)
        b_i = b_scr[pl.ds(base, GLA_SUB), :]
        r_i = b_i[0:1, :]
        q_t = (q_i * jnp.exp(b_i - r_i)).astype(BF16)
        k_t = (k_ref[...].astype(F32) * jnp.exp(jnp.minimum(r_i - b_scr[...], 0.0))).astype(BF16)
        p = jnp.where(lane < base, _nt_dot(q_t, k_t), 0.0)
        lane0 = base % LANES
        d = jnp.zeros((GLA_SUB, LANES), F32)
        for s in range(GLA_SUB):
            e = jnp.exp(jnp.minimum(b_i - b_i[s:s + 1, :], 0.0))
            c = jnp.sum(q_i * e * k_i[s:s + 1, :], axis=-1, keepdims=True)
            d = d + jnp.where(lane_h == lane0 + s, jnp.where(row_h >= s, c, 0.0), 0.0)
        halves = [jnp.where(base // LANES == h, d, 0.0) for h in range(tb // LANES)]
        amat_scr[pl.ds(base, GLA_SUB), :] = p + jnp.concatenate(halves, axis=1)
        return carry

    @pl.when(jnp.logical_not(mild))
    def _():
        lax.fori_loop(0, tb // GLA_SUB, sub_chunk, 0)

    o = o + jnp.dot(amat_scr[...].astype(BF16), v, preferred_element_type=F32)

    decay_col = jnp.broadcast_to(jnp.exp(b_last), (LANES, dk)).T
    dv = v.shape[1]
    decay_full = jnp.concatenate([decay_col] * (dv // LANES), axis=1)
    s_ref[...] = s_prev * decay_full + _tn_dot(k_hat, v)

    rgate = r_ref[...].astype(F32)
    a_ref[...] = (_rms(o, gn_ref[...]) * (rgate * _sigmoid(rgate))).astype(a_ref.dtype)


def gla_prompt(qkvr, ag, w2, gb, gnorm, batch, seq, dk, dv):
    heads = GLA_HEADS
    hg = GLA_HEADS_PER_STEP
    assert heads % hg == 0
    tb = min(GLA_BLOCK, seq)
    nblk = seq // tb
    row = lambda b, h, i: b * nblk + i
    wk, wv = hg * dk, hg * dv
    v0, r0 = (2 * heads * dk) // wv, (2 * heads * dk + heads * dv) // wv
    kern = functools.partial(_gla_prompt_kernel, tb=tb, dk=dk, dv=dv, heads=hg)
    return pl.pallas_call(
        kern,
        out_shape=[jax.ShapeDtypeStruct((batch * seq, heads * dv), BF16),
                   jax.ShapeDtypeStruct((batch, heads, dk, dv), F32)],
        grid=(batch, heads // hg, nblk),
        in_specs=[
            pl.BlockSpec((tb, wk), lambda b, h, i: (row(b, h, i), h)),
            pl.BlockSpec((tb, wk), lambda b, h, i: (row(b, h, i), heads // hg + h)),
            pl.BlockSpec((tb, wv), lambda b, h, i: (row(b, h, i), v0 + h)),
            pl.BlockSpec((tb, wv), lambda b, h, i: (row(b, h, i), r0 + h)),
            pl.BlockSpec((tb, LANES), lambda b, h, i: (row(b, h, i), 0)),
            pl.BlockSpec((LANES, wk), lambda b, h, i: (0, h)),
            pl.BlockSpec((1, wk), lambda b, h, i: (0, h)),
            pl.BlockSpec((1, dv), lambda b, h, i: (0, 0)),
        ],
        out_specs=[pl.BlockSpec((tb, wv), lambda b, h, i: (row(b, h, i), h)),
                   pl.BlockSpec((1, hg, dk, dv), lambda b, h, i: (b, h, 0, 0))],
        scratch_shapes=[pltpu.VMEM((hg, tb, dk), F32), pltpu.VMEM((hg, tb, tb), F32)],
        compiler_params=_cparams("parallel", "parallel", "arbitrary"),
        name="gla_prompt",
    )(qkvr, qkvr, qkvr, qkvr, ag, w2, gb.reshape(1, -1), gnorm.reshape(1, -1))


def _gla_step_kernel(q_ref, k_ref, v_ref, r_ref, ag_ref, w2_ref, gb_ref, gn_ref, s0_ref, a_ref, s_ref, *, nb, dk):
    g = _gla_log_decay(ag_ref[...], w2_ref[...], gb_ref[...])
    reps = LANES // nb
    decay = jnp.exp(g)
    decay_t = jnp.concatenate([decay] * reps, axis=0).T
    k = k_ref[...]
    k_t = jnp.concatenate([k] * reps, axis=0).T
    q = q_ref[...] * (dk ** -0.5)
    q_dec = (q * decay).astype(BF16)
    v = v_ref[...]
    rows = lax.broadcasted_iota(jnp.int32, (nb, v.shape[1]), 0)
    o = jnp.sum(q * k, axis=-1, keepdims=True) * v
    for j in range(nb):
        s0 = s0_ref[j, 0]
        s_ref[j, 0] = s0 * decay_t[:, j:j + 1] + k_t[:, j:j + 1] * v[j:j + 1, :]
        o = o + jnp.where(rows == j, jnp.dot(q_dec, s0.astype(BF16), preferred_element_type=F32), 0.0)
    rgate = r_ref[...]
    a_ref[...] = _rms(o, gn_ref[...]) * (rgate * _sigmoid(rgate))


def gla_step(proj, ag, w2, gb, gnorm, s0, dk, dv):
    heads = GLA_HEADS
    batch = proj.shape[0]
    nb = 8
    kern = functools.partial(_gla_step_kernel, nb=nb, dk=dk)
    return pl.pallas_call(
        kern,
        out_shape=[jax.ShapeDtypeStruct((batch, heads * dv), F32),
                   jax.ShapeDtypeStruct((batch, heads, dk, dv), F32)],
        grid=(batch // nb, heads),
        in_specs=[
            pl.BlockSpec((nb, dk), lambda b, h: (b, h)),
            pl.BlockSpec((nb, dk), lambda b, h: (b, heads + h)),
            pl.BlockSpec((nb, dv), lambda b, h: (b, (2 * heads * dk) // dv + h)),
            pl.BlockSpec((nb, dv), lambda b, h: (b, (2 * heads * dk) // dv + heads + h)),
            pl.BlockSpec((nb, LANES), lambda b, h: (b, 0)),
            pl.BlockSpec((LANES, dk), lambda b, h: (0, h)),
            pl.BlockSpec((1, dk), lambda b, h: (0, h)),
            pl.BlockSpec((1, dv), lambda b, h: (0, 0)),
            pl.BlockSpec((nb, 1, dk, dv), lambda b, h: (b, h, 0, 0)),
        ],
        out_specs=[pl.BlockSpec((nb, dv), lambda b, h: (b, h)),
                   pl.BlockSpec((nb, 1, dk, dv), lambda b, h: (b, h, 0, 0))],
        compiler_params=_cparams("parallel", "parallel"),
        name="gla_step",
    )(proj, proj, proj, proj, ag, w2, gb.reshape(1, -1), gnorm.reshape(1, -1), s0)


def _rel_bucket(dist):
    n = jnp.maximum(dist, 0)
    max_exact = REL_BUCKETS // 2
    large = max_exact + (jnp.log(jnp.maximum(n, 1).astype(F32) / max_exact)
                         / math.log(REL_MAX_DIST / max_exact) * (REL_BUCKETS - max_exact)).astype(jnp.int32)
    large = jnp.minimum(large, REL_BUCKETS - 1)
    return jnp.where(n < max_exact, n, large)


def _lane_tile(x, width):
    return jnp.concatenate([x] * (width // LANES), axis=1)


def _diff_prompt_kernel(lam_ref, q_ref, k_ref, v_ref, w_ref, far_ref, dn_ref, o_ref,
                        m_scr, l_scr, corr_scr, acc_scr, bias_scr, s_scr, p_scr, *, tq, dh, out_scale):
    qi = pl.program_id(2)
    h = pl.program_id(1)
    far = far_ref[h]

    @pl.when(qi == 0)
    def _():
        toep = pltpu.roll(jnp.broadcast_to(w_ref[0], (tq, tq)), 0, 1, stride=1, stride_axis=0)
        row = lax.broadcasted_iota(jnp.int32, (tq, tq), 0)
        col = lax.broadcasted_iota(jnp.int32, (tq, tq), 1)
        bias_scr[0] = jnp.where(col <= row, toep, NEG_BIG)
        bias_scr[1] = jnp.where(col > row, toep, far)

    m_scr[...] = jnp.full_like(m_scr, NEG_BIG)
    l_scr[...] = jnp.zeros_like(l_scr)
    acc_scr[...] = jnp.zeros_like(acc_scr)
    q = q_ref[...]

    def visit(kb, bias_rows):
        start = pl.multiple_of(kb * tq, tq)
        kk = k_ref[pl.ds(start, tq), :].astype(BF16)
        vv = v_ref[pl.ds(start, tq), :].astype(BF16)
        for c in range(2):
            s_scr[c] = _nt_dot(q[:, c * dh:(c + 1) * dh], kk[:, c * dh:(c + 1) * dh])

        def chunk(r, carry):
            rows = pl.ds(r * ATTN_ROWS, ATTN_ROWS)
            bias = bias_rows(rows)
            for c in range(2):
                s = s_scr[c, rows, :] + bias
                m_old = m_scr[c, rows, :]
                m_new = jnp.maximum(m_old, jnp.max(s, axis=-1, keepdims=True))
                corr = jnp.exp2(m_old - m_new)
                p = jnp.exp2(s - _lane_tile(m_new, tq))
                l_scr[c, rows, :] = l_scr[c, rows, :] * corr + jnp.sum(p, axis=-1, keepdims=True)
                corr_scr[c, rows, :] = corr
                m_scr[c, rows, :] = m_new
                p_scr[c, rows, :] = p.astype(BF16)
            return carry

        for r in range(tq // ATTN_ROWS):
            chunk(r, 0)
        for c in range(2):
            acc_scr[c] = (acc_scr[c] * _lane_tile(corr_scr[c], 2 * dh)
                          + jnp.dot(p_scr[c], vv, preferred_element_type=F32))

    def far_block(kb, carry):
        visit(kb, lambda rows: far)
        return carry

    lax.fori_loop(0, jnp.maximum(qi - 1, 0), far_block, 0)

    @pl.when(qi >= 1)
    def _():
        visit(qi - 1, lambda rows: bias_scr[1, rows, :])

    visit(qi, lambda rows: bias_scr[0, rows, :])
    lam = lam_ref[0]
    o = (acc_scr[0] / _lane_tile(l_scr[0], 2 * dh) - lam * (acc_scr[1] / _lane_tile(l_scr[1], 2 * dh)))
    o_ref[...] = (_rms(o, dn_ref[...]) * out_scale).astype(o_ref.dtype)


def diff_attention_prompt(q, k, v, lam, rel_table, diff_norm, batch, seq, lam_init):
    heads = DIFF_HEADS
    dh = q.shape[1] // (2 * heads)
    tq = min(ATTN_BLOCK, seq)
    assert tq >= REL_MAX_DIST or seq == tq
    nq = seq // tq
    dist = (tq - jnp.arange(tq, dtype=jnp.int32)) % tq
    w = (rel_table[_rel_bucket(dist)].astype(F32) * LOG2E).T.reshape(heads, 1, tq)
    far = rel_table[REL_BUCKETS - 1].astype(F32) * LOG2E
    kern = functools.partial(_diff_prompt_kernel, tq=tq, dh=dh, out_scale=1.0 - lam_init)
    return pl.pallas_call(
        kern,
        out_shape=jax.ShapeDtypeStruct((batch * seq, heads * 2 * dh), BF16),
        grid=(batch, heads, nq),
        in_specs=[
            pl.BlockSpec(memory_space=pltpu.SMEM),
            pl.BlockSpec((tq, 2 * dh), lambda b, h, i: (b * nq + i, h)),
            pl.BlockSpec((seq, 2 * dh), lambda b, h, i: (b, h)),
            pl.BlockSpec((seq, 2 * dh), lambda b, h, i: (b, h)),
            pl.BlockSpec((1, 1, tq), lambda b, h, i: (h, 0, 0)),
            pl.BlockSpec(memory_space=pltpu.SMEM),
            pl.BlockSpec((1, 2 * dh), lambda b, h, i: (0, 0)),
        ],
        out_specs=pl.BlockSpec((tq, 2 * dh), lambda b, h, i: (b * nq + i, h)),
        scratch_shapes=[pltpu.VMEM((2, tq, LANES), F32), pltpu.VMEM((2, tq, LANES), F32),
                        pltpu.VMEM((2, tq, LANES), F32), pltpu.VMEM((2, tq, 2 * dh), F32),
                        pltpu.VMEM((2, tq, tq), F32), pltpu.VMEM((2, tq, tq), F32), pltpu.VMEM((2, tq, tq), BF16)],
        compiler_params=_cparams("arbitrary", "arbitrary", "arbitrary"),
        name="diff_attention_prompt",
    )(lam.reshape(1), q, k, v, w, far, diff_norm.reshape(1, -1))


def _diff_paged_kernel(pt_ref, lam_ref, q_ref, *rest, n_pages, pages, out_scale, heads):
    k_refs = rest[:pages]
    v_refs = rest[pages:2 * pages]
    bias_ref, kn_ref, vn_ref, bn_ref, dn_ref, o_ref, m_scr, l_scr, acc_scr = rest[2 * pages:]
    del pt_ref
    jg = pl.program_id(1)

    @pl.when(jg == 0)
    def _():
        m_scr[...] = jnp.full_like(m_scr, NEG_BIG)
        l_scr[...] = jnp.zeros_like(l_scr)
        acc_scr[...] = jnp.zeros_like(acc_scr)

    q = q_ref[0]

    def update(ks, vs, biases):
        ss = [_nt_dot(q, kk.astype(BF16)) + bias for kk, bias in zip(ks, biases)]
        m_old = m_scr[...]
        m_new = m_old
        for s in ss:
            m_new = jnp.maximum(m_new, jnp.max(s, axis=-1, keepdims=True))
        corr = jnp.exp(m_old - m_new)
        l_new = l_scr[...] * corr
        acc = acc_scr[...] * corr
        for s, vv in zip(ss, vs):
            p = jnp.exp(s - m_new)
            l_new = l_new + jnp.sum(p, axis=-1, keepdims=True)
            acc = acc + jnp.dot(p.astype(BF16), vv.astype(BF16), preferred_element_type=F32)
        l_scr[...] = l_new
        acc_scr[...] = acc
        m_scr[...] = m_new

    def as_rows(ref):
        x = ref[0, 0]
        return x.reshape(x.shape[0] * x.shape[1], x.shape[2])

    biases = [bias_ref[(jg * pages + i == n_pages - 1).astype(jnp.int32)] for i in range(pages)]
    update([as_rows(r) for r in k_refs], [as_rows(r) for r in v_refs], biases)

    @pl.when(jg == pl.num_programs(1) - 1)
    def _():
        update([kn_ref[0]], [vn_ref[0]], [bn_ref[...]])
        o = acc_scr[...] / l_scr[...]
        out = o[:heads] - lam_ref[0] * o[heads:]
        o_ref[0] = _rms(out, dn_ref[...]) * out_scale


def diff_attention_paged(q, k_new, v_new, lam, cache_k, cache_v, layer, page_table, rel_table, diff_norm, lam_init):
    heads = DIFF_HEADS
    batch = q.shape[0]
    dh = q.shape[1] // (2 * heads)
    n_pages = page_table.shape[1]
    page = cache_k.shape[2]
    past = n_pages * page
    pages = math.gcd(PAGES_PER_STEP, n_pages)
    assert page >= REL_MAX_DIST
    rows = page * heads
    q4 = (q.reshape(batch, heads, 2, dh) * (dh ** -0.5)).astype(BF16)
    zeros = jnp.zeros_like(q4[:, :, 0])
    q_mat = jnp.concatenate([jnp.concatenate([q4[:, :, 0], zeros], axis=-1),
                             jnp.concatenate([zeros, q4[:, :, 1]], axis=-1)], axis=1)
    tok = jnp.arange(page, dtype=jnp.int32)
    dist_last = past - ((n_pages - 1) * page + tok)
    b_far = jnp.broadcast_to(rel_table[REL_BUCKETS - 1].astype(F32)[:, None], (heads, page))
    b_last = rel_table[_rel_bucket(dist_last)].astype(F32).T
    own = (jnp.arange(heads)[:, None, None] == jnp.arange(heads)[None, None, :])
    def expand(bh):
        t = jnp.where(own, bh[:, :, None], NEG_BIG).reshape(heads, rows)
        return jnp.concatenate([t, t], axis=0)
    bias = jnp.stack([expand(b_far), expand(b_last)])
    pad = LANES - heads
    kn = jnp.pad(k_new.reshape(batch, heads, 2 * dh), ((0, 0), (0, pad), (0, 0)))
    vn = jnp.pad(v_new.reshape(batch, heads, 2 * dh), ((0, 0), (0, pad), (0, 0)))
    b0 = jnp.where(jnp.arange(heads)[:, None] == jnp.arange(LANES)[None, :],
                   rel_table[0].astype(F32)[:, None], NEG_BIG)
    bn = jnp.concatenate([b0, b0], axis=0)

    def page_spec(i):
        return pl.BlockSpec((1, 1) + cache_k.shape[2:],
                            lambda b, j, pt: (layer, pt[b * n_pages + j * pages + i], 0, 0, 0))

    kern = functools.partial(_diff_paged_kernel, n_pages=n_pages, pages=pages, out_scale=1.0 - lam_init, heads=heads)
    out = pl.pallas_call(
        kern,
        out_shape=jax.ShapeDtypeStruct((batch, heads, 2 * dh), F32),
        grid_spec=pltpu.PrefetchScalarGridSpec(
            num_scalar_prefetch=1,
            grid=(batch, n_pages // pages),
            in_specs=[pl.BlockSpec(memory_space=pltpu.SMEM),
                      pl.BlockSpec((1, 2 * heads, 2 * dh), lambda b, j, pt: (b, 0, 0))]
                     + [page_spec(i) for i in range(pages)] + [page_spec(i) for i in range(pages)]
                     + [pl.BlockSpec((2, 2 * heads, rows), lambda b, j, pt: (0, 0, 0)),
                        pl.BlockSpec((1, LANES, 2 * dh), lambda b, j, pt: (b, 0, 0)),
                        pl.BlockSpec((1, LANES, 2 * dh), lambda b, j, pt: (b, 0, 0)),
                        pl.BlockSpec((2 * heads, LANES), lambda b, j, pt: (0, 0)),
                        pl.BlockSpec((1, 2 * dh), lambda b, j, pt: (0, 0))],
            out_specs=pl.BlockSpec((1, heads, 2 * dh), lambda b, j, pt: (b, 0, 0)),
            scratch_shapes=[pltpu.VMEM((2 * heads, 1), F32), pltpu.VMEM((2 * heads, 1), F32),
                            pltpu.VMEM((2 * heads, 2 * dh), F32)],
        ),
        compiler_params=_cparams("parallel", "arbitrary"),
        name="diff_attention_paged",
    )(page_table.reshape(-1), lam.reshape(1), q_mat, *([cache_k] * pages), *([cache_v] * pages),
      bias, kn, vn, bn, diff_norm.reshape(1, -1))
    return out.reshape(batch, heads * 2 * dh)


def _mem_attn_kernel(q_ref, k_ref, v_ref, o_ref, *, scale):
    q = q_ref[0]
    tq = q.shape[0]
    if tq < 8:
        q = jnp.broadcast_to(q[0:1], (8, q.shape[1]))
    s = _nt_dot(q.astype(BF16), k_ref[0].astype(BF16)) * scale
    p = jnp.exp(s - jnp.max(s, axis=-1, keepdims=True))
    o = jnp.dot(p.astype(BF16), v_ref[0].astype(BF16), preferred_element_type=F32)
    o = o / jnp.sum(p, axis=-1, keepdims=True)
    o_ref[0] = o[:tq].astype(o_ref.dtype)


def memory_attention(q3, col0, mem_k, mem_v, out_dtype, tq=2048):
    batch, seq, _ = q3.shape
    mtok, width = mem_k.shape[1:]
    md = width // MEM_HEADS
    tq = min(tq, seq)
    kern = functools.partial(_mem_attn_kernel, scale=md ** -0.5)
    return pl.pallas_call(
        kern,
        out_shape=jax.ShapeDtypeStruct((batch, seq, width), out_dtype),
        grid=(batch, MEM_HEADS, seq // tq),
        in_specs=[pl.BlockSpec((1, tq, md), lambda b, h, i: (b, i, col0 + h)),
                  pl.BlockSpec((1, mtok, md), lambda b, h, i: (b, 0, h)),
                  pl.BlockSpec((1, mtok, md), lambda b, h, i: (b, 0, h))],
        out_specs=pl.BlockSpec((1, tq, md), lambda b, h, i: (b, i, h)),
        compiler_params=_cparams("parallel", "parallel", "parallel"),
        name="memory_attention",
    )(q3, mem_k, mem_v)


def _mem_step_kernel(q_ref, k_ref, v_ref, o_ref, *, scale, heads):
    kk = k_ref[0, 0]
    rows = kk.shape[0] * kk.shape[1]
    k2 = kk.reshape(rows, kk.shape[2]).astype(BF16)
    v2 = v_ref[0, 0].reshape(rows, kk.shape[2]).astype(BF16)
    s = _nt_dot(q_ref[0].astype(BF16), k2) * scale
    lane = lax.broadcasted_iota(jnp.int32, s.shape, 1)
    qrow = lax.broadcasted_iota(jnp.int32, s.shape, 0)
    s = jnp.where(lane % heads == qrow, s, NEG_BIG)
    p = jnp.exp(s - jnp.max(s, axis=-1, keepdims=True))
    o = jnp.dot(p.astype(BF16), v2, preferred_element_type=F32)
    o_ref[0] = o / jnp.sum(p, axis=-1, keepdims=True)


def memory_attention_step(q, cache_k, cache_v, layer):
    batch = q.shape[0]
    _, _, mtok, heads, md = cache_k.shape
    qrows = 8
    assert heads <= qrows
    q8 = jnp.pad(q.reshape(batch, heads, md), ((0, 0), (0, qrows - heads), (0, 0)))
    kern = functools.partial(_mem_step_kernel, scale=md ** -0.5, heads=heads)
    cache_spec = pl.BlockSpec((1, 1, mtok, heads, md), lambda b: (layer, b, 0, 0, 0))
    out = pl.pallas_call(
        kern,
        out_shape=jax.ShapeDtypeStruct((batch, qrows, md), F32),
        grid=(batch,),
        in_specs=[pl.BlockSpec((1, qrows, md), lambda b: (b, 0, 0)), cache_spec, cache_spec],
        out_specs=pl.BlockSpec((1, qrows, md), lambda b: (b, 0, 0)),
        compiler_params=_cparams("parallel"),
        name="memory_attention_step",
    )(q8, cache_k, cache_v)
    return out[:, :heads].reshape(batch, heads * md)


def _merge_kernel(a_ref, b_ref, c_ref, g0_ref, g1_ref, g2_ref, x_ref, w_ref, npost_ref, npre_ref, x1_ref, h2_ref):
    tm = x_ref.shape[0]
    sub = math.gcd(tm, MERGE_SUB_ROWS)
    for rb in range(tm // sub):
        rs = pl.ds(rb * sub, sub)
        merged = 0.5 * ((jnp.tanh(0.5 * g0_ref[rs, :].astype(F32)) + 1.0) * a_ref[rs, :].astype(F32)
                        + (jnp.tanh(0.5 * g1_ref[rs, :].astype(F32)) + 1.0) * b_ref[rs, :].astype(F32)
                        + (jnp.tanh(0.5 * g2_ref[rs, :].astype(F32)) + 1.0) * c_ref[rs, :].astype(F32))
        y = jnp.dot(merged.astype(BF16), w_ref[...], preferred_element_type=F32)
        x1 = x_ref[rs, :] + _rms(y, npost_ref[...])
        x1_ref[rs, :] = x1
        h2_ref[rs, :] = _rms(x1, npre_ref[...]).astype(h2_ref.dtype)


def merge_out_proj(a, b, c, gates_arr, gate_col0, x, w_out, norm_post, norm_pre_ffn, tm=256):
    m, d = x.shape
    tm = min(tm, m)
    rowblk = lambda i: (i, 0)
    vec = pl.BlockSpec((1, d), lambda i: (0, 0))
    return pl.pallas_call(
        _merge_kernel,
        out_shape=[jax.ShapeDtypeStruct((m, d), F32), jax.ShapeDtypeStruct((m, d), BF16)],
        grid=(m // tm,),
        in_specs=[pl.BlockSpec((tm, d), rowblk), pl.BlockSpec((tm, d), rowblk), pl.BlockSpec((tm, d), rowblk),
                  pl.BlockSpec((tm, d), lambda i: (i, gate_col0)),
                  pl.BlockSpec((tm, d), lambda i: (i, gate_col0 + 1)),
                  pl.BlockSpec((tm, d), lambda i: (i, gate_col0 + 2)),
                  pl.BlockSpec((tm, d), rowblk),
                  pl.BlockSpec((d, d), lambda i: (0, 0)),
                  vec, vec],
        out_specs=[pl.BlockSpec((tm, d), rowblk), pl.BlockSpec((tm, d), rowblk)],
        compiler_params=_cparams("parallel"),
        name="merge_out_proj",
    )(a, b, c, gates_arr, gates_arr, gates_arr, x, w_out, norm_post.reshape(1, d), norm_pre_ffn.reshape(1, d))


def _gelu_tanh(x):
    return 0.5 * x * (1.0 + jnp.tanh(math.sqrt(2.0 / math.pi) * (x + 0.044715 * (x * x * x))))


def _ffn_up_seq_kernel(h_ref, wg_ref, wv_ref, cwg_ref, cwv_ref, cbg_ref, cbv_ref, stg_ref, stv_ref,
                       act_ref, tailg_ref, tailv_ref, carryg, carryv, *, tiles_per_seq):
    mi = pl.program_id(1)
    h = h_ref[...]
    tm = h.shape[0]
    row = lax.broadcasted_iota(jnp.int32, (tm, 1), 0)
    seq_start = mi % tiles_per_seq == 0

    def conv(w_ref, cw_ref, cb_ref, st_ref, carry, tail_ref):
        u = jnp.dot(h, w_ref[...], preferred_element_type=F32)
        st = st_ref[0]
        prev2 = jnp.where(seq_start, st[0:1, :], carry[6:7, :])
        prev1 = jnp.where(seq_start, st[1:2, :], carry[7:8, :])
        u1 = jnp.where(row >= 1, pltpu.roll(u, 1, 0), prev1)
        u2 = jnp.where(row >= 2, pltpu.roll(u, 2, 0), jnp.where(row == 1, prev1, prev2))
        cw = cw_ref[...]
        tail = u[tm - 8:tm, :]
        carry[...] = tail
        tail_ref[0] = tail
        return cb_ref[...] + u2 * cw[0:1, :] + u1 * cw[1:2, :] + u * cw[2:3, :]

    gate = conv(wg_ref, cwg_ref, cbg_ref, stg_ref, carryg, tailg_ref)
    val = conv(wv_ref, cwv_ref, cbv_ref, stv_ref, carryv, tailv_ref)
    act_ref[...] = (_gelu_tanh(gate) * val).astype(act_ref.dtype)


def ffn_up_seq(h2, w_up, conv_w, conv_b, conv_state, seq, tm=1024, tn=512):
    m, d = h2.shape
    f = w_up.shape[1] // 2
    tm = min(tm, seq)
    tiles_per_seq = seq // tm
    nj = f // tn
    kern = functools.partial(_ffn_up_seq_kernel, tiles_per_seq=tiles_per_seq)
    cw = conv_w.astype(F32)
    cb = conv_b.reshape(1, -1).astype(F32)
    return pl.pallas_call(
        kern,
        out_shape=[jax.ShapeDtypeStruct((m, f), BF16),
                   jax.ShapeDtypeStruct((m // tm, 8, f), F32),
                   jax.ShapeDtypeStruct((m // tm, 8, f), F32)],
        grid=(nj, m // tm),
        in_specs=[pl.BlockSpec((tm, d), lambda j, i: (i, 0)),
                  pl.BlockSpec((d, tn), lambda j, i: (0, j)),
                  pl.BlockSpec((d, tn), lambda j, i: (0, nj + j)),
                  pl.BlockSpec((cw.shape[0], tn), lambda j, i: (0, j)),
                  pl.BlockSpec((cw.shape[0], tn), lambda j, i: (0, nj + j)),
                  pl.BlockSpec((1, tn), lambda j, i: (0, j)),
                  pl.BlockSpec((1, tn), lambda j, i: (0, nj + j)),
                  pl.BlockSpec((1, 2, tn), lambda j, i: (i // tiles_per_seq, 0, j)),
                  pl.BlockSpec((1, 2, tn), lambda j, i: (i // tiles_per_seq, 0, nj + j))],
        out_specs=[pl.BlockSpec((tm, tn), lambda j, i: (i, j)),
                   pl.BlockSpec((1, 8, tn), lambda j, i: (i, 0, j)),
                   pl.BlockSpec((1, 8, tn), lambda j, i: (i, 0, j))],
        scratch_shapes=[pltpu.VMEM((8, tn), F32), pltpu.VMEM((8, tn), F32)],
        compiler_params=_cparams("parallel", "arbitrary"),
        name="ffn_up_seq",
    )(h2, w_up, w_up, cw, cw, cb, cb, conv_state, conv_state)


def _ffn_up_step_kernel(h_ref, wg_ref, wv_ref, cwg_ref, cwv_ref, cbg_ref, cbv_ref, stg_ref, stv_ref,
                        act_ref, ug_ref, uv_ref):
    h = h_ref[...]

    def conv(w_ref, cw_ref, cb_ref, st_ref, u_ref):
        u = jnp.dot(h, w_ref[...], preferred_element_type=F32)
        u_ref[...] = u
        cw = cw_ref[...]
        return cb_ref[...] + st_ref[0] * cw[0:1, :] + st_ref[1] * cw[1:2, :] + u * cw[2:3, :]

    gate = conv(wg_ref, cwg_ref, cbg_ref, stg_ref, ug_ref)
    val = conv(wv_ref, cwv_ref, cbv_ref, stv_ref, uv_ref)
    act_ref[...] = (_gelu_tanh(gate) * val).astype(act_ref.dtype)


def ffn_up_step(h2, w_up, conv_w, conv_b, conv_state_t, tn=512):
    m, d = h2.shape
    f = w_up.shape[1] // 2
    nj = f // tn
    cw = conv_w.astype(F32)
    cb = conv_b.reshape(1, -1).astype(F32)
    return pl.pallas_call(
        _ffn_up_step_kernel,
        out_shape=[jax.ShapeDtypeStruct((m, f), BF16), jax.ShapeDtypeStruct((m, f), F32),
                   jax.ShapeDtypeStruct((m, f), F32)],
        grid=(nj,),
        in_specs=[pl.BlockSpec((m, d), lambda j: (0, 0)),
                  pl.BlockSpec((d, tn), lambda j: (0, j)),
                  pl.BlockSpec((d, tn), lambda j: (0, nj + j)),
                  pl.BlockSpec((cw.shape[0], tn), lambda j: (0, j)),
                  pl.BlockSpec((cw.shape[0], tn), lambda j: (0, nj + j)),
                  pl.BlockSpec((1, tn), lambda j: (0, j)),
                  pl.BlockSpec((1, tn), lambda j: (0, nj + j)),
                  pl.BlockSpec((2, m, tn), lambda j: (0, 0, j)),
                  pl.BlockSpec((2, m, tn), lambda j: (0, 0, nj + j))],
        out_specs=[pl.BlockSpec((m, tn), lambda j: (0, j)),
                   pl.BlockSpec((m, tn), lambda j: (0, j)),
                   pl.BlockSpec((m, tn), lambda j: (0, j))],
        compiler_params=_cparams("parallel"),
        name="ffn_up_step",
    )(h2, w_up, w_up, cw, cw, cb, cb, conv_state_t, conv_state_t)


def _ffn_down_kernel(act_ref, w_ref, x_ref, n_ref, y_ref):
    f = jnp.dot(act_ref[...], w_ref[...], preferred_element_type=F32)
    y_ref[...] = x_ref[...] + _rms(f, n_ref[...])


def ffn_down(act, w_down, x1, norm_post, tm=512):
    m, f = act.shape
    d = w_down.shape[1]
    tm = min(tm, m)
    assert m % tm == 0
    return pl.pallas_call(
        _ffn_down_kernel,
        out_shape=jax.ShapeDtypeStruct((m, d), F32),
        grid=(m // tm,),
        in_specs=[pl.BlockSpec((tm, f), lambda i: (i, 0)),
                  pl.BlockSpec((f, d), lambda i: (0, 0), pipeline_mode=pl.Buffered(1)),
                  pl.BlockSpec((tm, d), lambda i: (i, 0)),
                  pl.BlockSpec((1, d), lambda i: (0, 0))],
        out_specs=pl.BlockSpec((tm, d), lambda i: (i, 0)),
        compiler_params=_cparams("parallel"),
        name="ffn_down",
    )(act, w_down, x1, norm_post.reshape(1, d))


def kernel(x_prompt, x_sample, cache_k, cache_v, state_gla, cache_mem_k, cache_mem_v, state_ffn_conv, page_table, mem_prompt, rel_bias_table, norm_pre_mix, norm_post_mix, norm_pre_ffn, norm_post_ffn, norm_mem, w_in, w_gla_gate2, b_gla_gate, gla_norm, diff_lambda_q1, diff_lambda_k1, diff_lambda_q2, diff_lambda_k2, diff_norm, w_mem_kv, w_out, w_up, ffn_conv_w, ffn_conv_b, w_down):
    depth = w_in.shape[0]
    bp, tp, d = x_prompt.shape
    bs, ts, _ = x_sample.shape
    assert ts == 1
    dk, dv = d // (2 * GLA_HEADS), d // GLA_HEADS
    mtok = mem_prompt.shape[1]
    f2 = w_up.shape[2]

    yp = x_prompt.reshape(bp * tp, d)
    ys = x_sample.reshape(bs * ts, d)
    outs = {n: [] for n in ("kp", "vp", "gp", "mk", "mv", "cp", "ks", "vs", "gs", "cs")}
    for l in range(depth):
        lam_init = 0.8 - 0.6 * math.exp(-0.3 * l)
        lam = (jnp.exp(jnp.sum(diff_lambda_q1[l].astype(F32) * diff_lambda_k1[l].astype(F32)))
               - jnp.exp(jnp.sum(diff_lambda_q2[l].astype(F32) * diff_lambda_k2[l].astype(F32))) + lam_init)
        rank = w_gla_gate2.shape[1]
        w2 = jnp.pad(w_gla_gate2[l].astype(F32), ((0, LANES - rank), (0, 0)))
        c_ag = 3 * d
        c_qd = c_ag + rank
        c_kd, c_vd, c_qm = c_qd + d, c_qd + 2 * d, c_qd + 3 * d
        q_scale = (d // (2 * DIFF_HEADS)) ** -0.5 * LOG2E
        w_out_b, w_up_b, w_down_b = w_out[l].astype(BF16), w_up[l].astype(BF16), w_down[l].astype(BF16)
        w_in_t = jnp.swapaxes(w_in, 1, 2)
        proj = functools.partial(in_proj, w3=w_in_t, layer=l, by_rows=True)

        hs = rmsnorm_cast(ys, norm_pre_mix[l])
        (qkvr_s,) = proj(hs, col0=0, n=3 * d, out_dtypes=[F32])
        (ag_s,) = proj(hs, col0=c_ag, n=LANES, out_dtypes=[F32])
        (rest_s,) = proj(hs, col0=c_qd, n=7 * d, out_dtypes=[F32])
        kd_s, vd_s = rest_s[:, d:2 * d], rest_s[:, 2 * d:3 * d]
        a_s, gla_state_s = gla_step(qkvr_s, ag_s, w2, b_gla_gate[l], gla_norm[l], state_gla[l], dk, dv)
        b_s = diff_attention_paged(rest_s[:, :d], kd_s, vd_s, lam, cache_k, cache_v, l, page_table,
                                   rel_bias_table, diff_norm[l], lam_init)
        c_s = memory_attention_step(rest_s[:, 3 * d:4 * d], cache_mem_k, cache_mem_v, l)
        x1_s, h2_s = merge_out_proj(a_s, b_s, c_s, rest_s, 4, ys, w_out_b,
                                    norm_post_mix[l], norm_pre_ffn[l])
        st = state_ffn_conv[l].astype(F32)
        act_s, ug_s, uv_s = ffn_up_step(h2_s, w_up_b, ffn_conv_w[l], ffn_conv_b[l], jnp.swapaxes(st, 0, 1))
        ys = ffn_down(act_s, w_down_b, x1_s, norm_post_ffn[l])
        u_s = jnp.concatenate([ug_s, uv_s], axis=-1)
        outs["ks"].append(kd_s.reshape(bs, ts, DIFF_HEADS, -1))
        outs["vs"].append(vd_s.reshape(bs, ts, DIFF_HEADS, -1))
        outs["gs"].append(gla_state_s)
        outs["cs"].append(jnp.stack([st[:, 1], u_s], axis=1))

        hm = rmsnorm_cast(mem_prompt.reshape(bp * mtok, d), norm_mem[l])
        (mem_k,) = in_proj(hm, w_mem_kv, l, 0, d, [F32])
        (mem_v,) = in_proj(hm, w_mem_kv, l, d, d, [F32])
        h = rmsnorm_cast(yp, norm_pre_mix[l])
        (qkvr,) = proj(h, col0=0, n=3 * d, out_dtypes=[BF16])
        (ag,) = proj(h, col0=c_ag, n=LANES, out_dtypes=[F32])
        (qd,) = proj(h, col0=c_qd, n=d, out_dtypes=[BF16], out_scale=q_scale)
        (k_rows,) = proj(h, col0=c_kd, n=d, out_dtypes=[F32])
        (v_rows,) = proj(h, col0=c_vd, n=d, out_dtypes=[F32])
        (qm_gates,) = proj(h, col0=c_qm, n=4 * d, out_dtypes=[BF16])
        a, gla_state = gla_prompt(qkvr, ag, w2, b_gla_gate[l], gla_norm[l], bp, tp, dk, dv)
        b = diff_attention_prompt(qd, k_rows, v_rows, lam, rel_bias_table, diff_norm[l], bp, tp, lam_init)
        c = memory_attention(qm_gates.reshape(bp, tp, -1), 0, mem_k.reshape(bp, mtok, d),
                             mem_v.reshape(bp, mtok, d), BF16)
        x1, h2 = merge_out_proj(a, b, c.reshape(bp * tp, d), qm_gates, 1, yp, w_out_b, norm_post_mix[l],
                                norm_pre_ffn[l])
        conv0 = jnp.zeros((bp, ffn_conv_w.shape[1] - 1, f2), F32)
        act, tail_g, tail_v = ffn_up_seq(h2, w_up_b, ffn_conv_w[l], ffn_conv_b[l], conv0, tp)
        yp = ffn_down(act, w_down_b, x1, norm_post_ffn[l])
        tiles = tail_g.shape[0] // bp
        tail = jnp.concatenate([tail_g, tail_v], axis=-1).reshape(bp, tiles, 8, f2)
        outs["kp"].append(k_rows.reshape(bp, tp, DIFF_HEADS, -1))
        outs["vp"].append(v_rows.reshape(bp, tp, DIFF_HEADS, -1))
        outs["gp"].append(gla_state)
        outs["mk"].append(mem_k.reshape(bp, mtok, MEM_HEADS, -1))
        outs["mv"].append(mem_v.reshape(bp, mtok, MEM_HEADS, -1))
        outs["cp"].append(tail[:, -1, 6:8])

    st = lambda n: jnp.stack(outs[n])
    return (yp.reshape(bp, tp, d), ys.reshape(bs, ts, d), st("kp"), st("vp"), st("gp"), st("mk"), st("mv"),
            st("cp"), st("ks"), st("vs"), st("gs"), st("cs"))
```
